```python
import math
import jax, jax.numpy as jnp
from jax import lax
import numpy as np

D_MODEL = 1024
BATCH = 8
SEQ = 2048
DEPTH = 2

CTX_LEN = 256
GRID_W = 64
ATT_HEADS = 8
ATT_KV_HEADS = 2
HEAD_DIM = 64
Q_PER_KV = ATT_HEADS // ATT_KV_HEADS
ATT_WIDTH = ATT_HEADS * HEAD_DIM
KV_WIDTH = ATT_KV_HEADS * HEAD_DIM
ROPE_THETA = 10000.0
Q_BLOCK = 128
SSD_HEADS = 8
SSD_HEAD_DIM = 64
SSD_WIDTH = SSD_HEADS * SSD_HEAD_DIM
SSD_GROUPS = 2
SSD_STATE = 128
SSD_CHUNK = 128
D_CONV = 3
CONV_WIDTH = SSD_WIDTH + 2 * SSD_GROUPS * SSD_STATE
MIX_WIDTH = ATT_WIDTH + SSD_WIDTH
IN_SPLITS = (ATT_WIDTH, ATT_WIDTH + KV_WIDTH, ATT_WIDTH + 2 * KV_WIDTH,
             ATT_WIDTH + 2 * KV_WIDTH + SSD_WIDTH, ATT_WIDTH + 2 * KV_WIDTH + SSD_WIDTH + CONV_WIDTH)
IN_WIDTH = IN_SPLITS[-1] + 2 * SSD_HEADS
N_EXPERTS = 64
TOP_K = 8
N_EXPERT_GROUPS = 8
TOPK_GROUPS = 4
EXPERT_DIM = 256
SHARED_DIM = 256
ROUTED_SCALE = 2.5
MOE_BLOCK = 128
EPS = 1e-6

kernel_name = 'hybrid_attn_ssd_moe_ctx_prefix_dit'


def rms_norm(x, g):
    xf = x.astype(jnp.float32)
    y = xf * lax.rsqrt(jnp.mean(xf * xf, axis=-1, keepdims=True) + EPS)
    return (y * g.astype(jnp.float32)).astype(x.dtype)


def modulate(h, shift, scale):
    return h * (1 + scale) + shift


def axial_rope_tables(rows):
    row = jnp.repeat(jnp.arange(rows), GRID_W).astype(jnp.float32)
    col = jnp.tile(jnp.arange(GRID_W), rows).astype(jnp.float32)
    n_freq = HEAD_DIM // 4
    inv_freq = ROPE_THETA ** (-jnp.arange(n_freq, dtype=jnp.float32) / n_freq)
    ang = jnp.concatenate([row[:, None] * inv_freq, col[:, None] * inv_freq], axis=-1)
    return jnp.cos(ang), jnp.sin(ang)


def apply_rope(x, cos, sin):
    x1, x2 = jnp.split(x.astype(jnp.float32), 2, axis=-1)
    cs, sn = cos[None, :, None, :], sin[None, :, None, :]
    return jnp.concatenate([x1 * cs - x2 * sn, x1 * sn + x2 * cs], axis=-1).astype(x.dtype)


def softmax_attend(q, k, v):
    s = jnp.einsum('bqkgd,bskd->bkgqs', q, k).astype(jnp.float32) * (HEAD_DIM ** -0.5)
    p = jax.nn.softmax(s, axis=-1).astype(v.dtype)
    return jnp.einsum('bkgqs,bskd->bqkgd', p, v)


def latent_attention(q, k, v):
    b, s = q.shape[:2]
    nb = s // Q_BLOCK
    qb = q.reshape(b, nb, Q_BLOCK, ATT_KV_HEADS, Q_PER_KV, HEAD_DIM).transpose(1, 0, 2, 3, 4, 5)
    out = lax.map(lambda qi: softmax_attend(qi, k, v), qb)
    return out.transpose(1, 0, 2, 3, 4, 5).reshape(b, s, ATT_WIDTH)


def segsum(a):
    t = a.shape[-1]
    ae = jnp.broadcast_to(a[..., :, None], a.shape + (t,))
    cs = jnp.cumsum(jnp.where(jnp.tril(jnp.ones((t, t), bool), -1), ae, 0.0), axis=-2)
    return jnp.where(jnp.tril(jnp.ones((t, t), bool)), cs, -jnp.inf)


def ssd_scan(xh, dt, a, bg, cg, h0):
    b, l, h, p = xh.shape
    nc = l // SSD_CHUNK
    rep = SSD_HEADS // SSD_GROUPS
    bh = jnp.repeat(bg, rep, axis=2).reshape(b, nc, SSD_CHUNK, h, SSD_STATE)
    ch = jnp.repeat(cg, rep, axis=2).reshape(b, nc, SSD_CHUNK, h, SSD_STATE)
    xd = (xh * dt[..., None].astype(xh.dtype)).reshape(b, nc, SSD_CHUNK, h, p)
    la = (dt * a).astype(jnp.float32).reshape(b, nc, SSD_CHUNK, h).transpose(0, 3, 1, 2)
    la_cs = jnp.cumsum(la, axis=-1)
    y_diag = jnp.einsum('bclhn,bcshn,bhcls,bcshp->bclhp', ch, bh, jnp.exp(segsum(la)), xd)
    decay_to_end = jnp.exp(la_cs[..., -1:] - la_cs)
    states = jnp.einsum('bclhn,bhcl,bclhp->bchpn', bh, decay_to_end, xd)
    states = jnp.concatenate([h0[:, None].astype(states.dtype), states], axis=1)
    chunk_decay = jnp.exp(segsum(jnp.pad(la_cs[..., -1], ((0, 0), (0, 0), (1, 0)))))
    new_states = jnp.einsum('bhzc,bchpn->bzhpn', chunk_decay, states)
    states_in, h_final = new_states[:, :-1], new_states[:, -1]
    y_off = jnp.einsum('bclhn,bchpn,bhcl->bclhp', ch, states_in, jnp.exp(la_cs))
    y = (y_diag + y_off).reshape(b, l, h, p)
    return y.astype(xh.dtype), h_final


def depthwise_conv(u, w, bias):
    out = lax.conv_general_dilated(u, w[:, None, :], window_strides=(1,), padding='SAME',
                                   dimension_numbers=('NWC', 'WIO', 'NWC'), feature_group_count=u.shape[-1])
    return out + bias


def ssd_stream(z, xbc, dt_raw, conv_w, conv_b, dt_bias, a_log, d_skip, g_norm, h0_fwd, h0_bwd):
    b, l, _ = z.shape
    xbc = jax.nn.silu(depthwise_conv(xbc, conv_w, conv_b))
    xs, bg, cg = jnp.split(xbc, [SSD_WIDTH, SSD_WIDTH + SSD_GROUPS * SSD_STATE], axis=-1)
    xh = xs.reshape(b, l, SSD_HEADS, SSD_HEAD_DIM)
    bg = bg.reshape(b, l, SSD_GROUPS, SSD_STATE)
    cg = cg.reshape(b, l, SSD_GROUPS, SSD_STATE)
    dt = jax.nn.softplus(dt_raw.astype(jnp.float32).reshape(b, l, 2, SSD_HEADS) + dt_bias.astype(jnp.float32))
    a = -jnp.exp(a_log.astype(jnp.float32))
    y_f, h_f = ssd_scan(xh, dt[:, :, 0], a[0], bg, cg, h0_fwd)
    y_b, h_b = ssd_scan(xh[:, ::-1], dt[:, ::-1, 1], a[1], bg[:, ::-1], cg[:, ::-1], h0_bwd)
    y = y_f + y_b[:, ::-1] + xh * d_skip[:, None]
    y = y.reshape(b, l, SSD_WIDTH)
    return rms_norm(y * jax.nn.silu(z), g_norm), h_f, h_b


def mixer(hc, hl, cos, sin, w_in, w_out, q_norm, k_norm, conv_w, conv_b, dt_bias, a_log, d_skip, ssd_norm, ctx_out):
    b, lc, _ = hc.shape
    s = hl.shape[1]
    qc, kc, vc, zc, xbcc, dtc = jnp.split(hc @ w_in, IN_SPLITS, axis=-1)
    ql, kl, vl, zl, xbcl, dtl = jnp.split(hl @ w_in, IN_SPLITS, axis=-1)
    kc = rms_norm(kc.reshape(b, lc, ATT_KV_HEADS, HEAD_DIM), k_norm)
    vc = vc.reshape(b, lc, ATT_KV_HEADS, HEAD_DIM)
    kl = apply_rope(rms_norm(kl.reshape(b, s, ATT_KV_HEADS, HEAD_DIM), k_norm), cos, sin)
    vl = vl.reshape(b, s, ATT_KV_HEADS, HEAD_DIM)
    ql = apply_rope(rms_norm(ql.reshape(b, s, ATT_HEADS, HEAD_DIM), q_norm), cos, sin)
    ql = ql.reshape(b, s, ATT_KV_HEADS, Q_PER_KV, HEAD_DIM)
    k_all = jnp.concatenate([kc, kl], axis=1)
    v_all = jnp.concatenate([vc, vl], axis=1)
    att_l = latent_attention(ql, k_all, v_all)
    zero = jnp.zeros((b, SSD_HEADS, SSD_HEAD_DIM, SSD_STATE), jnp.float32)
    ssd_c, h_f, h_b = ssd_stream(zc, xbcc, dtc, conv_w, conv_b, dt_bias, a_log, d_skip, ssd_norm, zero, zero)
    ssd_l, _, _ = ssd_stream(zl, xbcl, dtl, conv_w, conv_b, dt_bias, a_log, d_skip, ssd_norm, h_f, h_b)
    out_l = jnp.concatenate([att_l, ssd_l], axis=-1) @ w_out
    if not ctx_out:
        return None, out_l
    qc = rms_norm(qc.reshape(b, lc, ATT_HEADS, HEAD_DIM), q_norm).reshape(b, lc, ATT_KV_HEADS, Q_PER_KV, HEAD_DIM)
    att_c = softmax_attend(qc, kc, vc).reshape(b, lc, ATT_WIDTH)
    out_c = jnp.concatenate([att_c, ssd_c], axis=-1) @ w_out
    return out_c, out_l


def moe_ffn(h, router_w, router_bias, w_gate, w_up, w_down, ws_gate, ws_up, ws_down):
    b, l, d = h.shape
    scores = jax.nn.sigmoid(jnp.einsum('bld,de->ble', h, router_w).astype(jnp.float32))
    choice = scores + router_bias.astype(jnp.float32)
    grp = choice.reshape(b, l, N_EXPERT_GROUPS, N_EXPERTS // N_EXPERT_GROUPS)
    grp_score = lax.top_k(grp, 2)[0].sum(-1)
    _, grp_idx = lax.top_k(grp_score, TOPK_GROUPS)
    grp_mask = jax.nn.one_hot(grp_idx, N_EXPERT_GROUPS, dtype=jnp.float32).sum(-2)
    expert_ok = jnp.repeat(grp_mask, N_EXPERTS // N_EXPERT_GROUPS, axis=-1) > 0
    _, idx = lax.top_k(jnp.where(expert_ok, choice, -jnp.inf), TOP_K)
    gw = jnp.take_along_axis(scores, idx, axis=-1)
    gw = gw / jnp.sum(gw, axis=-1, keepdims=True) * ROUTED_SCALE
    combine = jnp.sum(jax.nn.one_hot(idx, N_EXPERTS, dtype=jnp.float32) * gw[..., None], axis=-2)
    nb = l // MOE_BLOCK
    hb = h.reshape(b, nb, MOE_BLOCK, d).swapaxes(0, 1)
    cb = combine.astype(h.dtype).reshape(b, nb, MOE_BLOCK, N_EXPERTS).swapaxes(0, 1)

    def block(args):
        hx, cx = args
        g = jnp.einsum('bld,edf->blef', hx, w_gate)
        u = jnp.einsum('bld,edf->blef', hx, w_up)
        return jnp.einsum('blef,efd->bld', jax.nn.silu(g) * u * cx[..., None], w_down)

    routed = lax.map(block, (hb, cb)).swapaxes(0, 1).reshape(b, l, d)
    shared = (jax.nn.silu(h @ ws_gate) * (h @ ws_up)) @ ws_down
    return routed + shared


def setup_inputs(seed: int = 0) -> dict:
    key = jax.random.key(seed)
    ks = jax.random.split(key, 32)
    f32 = jnp.float32
    L, D = DEPTH, D_MODEL

    def nrm(k, shape, scale):
        return jax.random.normal(k, shape, f32) * scale

    dt0 = jnp.exp(jax.random.uniform(ks[13], (L, 2, SSD_HEADS), f32, math.log(1e-3), math.log(1e-1)))
    return {
        'x': nrm(ks[0], (BATCH, SEQ, D), 1.0),
        'c': nrm(ks[1], (BATCH, D), 1.0),
        'ctx': nrm(ks[2], (BATCH, CTX_LEN, D), 1.0),
        'c_ctx': nrm(ks[3], (D,), 1.0),
        'w_mod': nrm(ks[4], (L, D, 6 * D), 0.5 * D ** -0.5),
        'b_mod': nrm(ks[5], (L, 6 * D), 0.01),
        'g_pre_mix': 1.0 + nrm(ks[6], (L, D), 0.05),
        'g_post_mix': 1.0 + nrm(ks[7], (L, D), 0.05),
        'g_pre_ffn': 1.0 + nrm(ks[8], (L, D), 0.05),
        'g_post_ffn': 1.0 + nrm(ks[9], (L, D), 0.05),
        'w_in': nrm(ks[10], (L, D, IN_WIDTH), D ** -0.5),
        'q_norm': 1.0 + nrm(ks[11], (L, HEAD_DIM), 0.05),
        'k_norm': 1.0 + nrm(ks[12], (L, HEAD_DIM), 0.05),
        'conv_w': nrm(ks[14], (L, D_CONV, CONV_WIDTH), D_CONV ** -0.5),
        'conv_b': nrm(ks[15], (L, CONV_WIDTH), 0.01),
        'dt_bias': dt0 + jnp.log(-jnp.expm1(-dt0)),
        'a_log': jnp.log(jax.random.uniform(ks[16], (L, 2, SSD_HEADS), f32, 1.0, 16.0)),
        'd_skip': 1.0 + nrm(ks[17], (L, SSD_HEADS), 0.05),
        'ssd_norm': 1.0 + nrm(ks[18], (L, SSD_WIDTH), 0.05),
        'w_out': nrm(ks[19], (L, MIX_WIDTH, D), MIX_WIDTH ** -0.5),
        'router_w': nrm(ks[20], (L, D, N_EXPERTS), D ** -0.5),
        'router_bias': nrm(ks[21], (L, N_EXPERTS), 0.01),
        'w_gate': nrm(ks[22], (L, N_EXPERTS, D, EXPERT_DIM), D ** -0.5),
        'w_up': nrm(ks[23], (L, N_EXPERTS, D, EXPERT_DIM), D ** -0.5),
        'w_down': nrm(ks[24], (L, N_EXPERTS, EXPERT_DIM, D), EXPERT_DIM ** -0.5),
        'ws_gate': nrm(ks[25], (L, D, SHARED_DIM), D ** -0.5),
        'ws_up': nrm(ks[26], (L, D, SHARED_DIM), D ** -0.5),
        'ws_down': nrm(ks[27], (L, SHARED_DIM, D), SHARED_DIM ** -0.5),
    }


def reference(x, c, ctx, c_ctx, w_mod, b_mod, g_pre_mix, g_post_mix, g_pre_ffn, g_post_ffn, w_in, q_norm, k_norm,
              conv_w, conv_b, dt_bias, a_log, d_skip, ssd_norm, w_out, router_w, router_bias, w_gate, w_up, w_down,
              ws_gate, ws_up, ws_down):
    s = x.shape[1]
    rows = s // GRID_W
    cos, sin = axial_rope_tables(rows)
    xl, xc = x, ctx
    for i in range(DEPTH):
        ctx_out = i < DEPTH - 1
        mod_l = (jax.nn.silu(c) @ w_mod[i] + b_mod[i])[:, None, :]
        mod_c = jax.nn.silu(c_ctx) @ w_mod[i] + b_mod[i]
        sh1_l, sc1_l, g1_l, sh2_l, sc2_l, g2_l = jnp.split(mod_l, 6, axis=-1)
        sh1_c, sc1_c, g1_c, sh2_c, sc2_c, g2_c = jnp.split(mod_c, 6, axis=-1)
        hl = modulate(rms_norm(xl, g_pre_mix[i]), sh1_l, sc1_l)
        hc = modulate(rms_norm(xc, g_pre_mix[i]), sh1_c, sc1_c)
        mc, ml = mixer(hc, hl, cos, sin, w_in[i], w_out[i], q_norm[i], k_norm[i], conv_w[i], conv_b[i],
                       dt_bias[i], a_log[i], d_skip[i], ssd_norm[i], ctx_out)
        xl = xl + g1_l * rms_norm(ml, g_post_mix[i])
        hl = modulate(rms_norm(xl, g_pre_ffn[i]), sh2_l, sc2_l)
        xl = xl + g2_l * rms_norm(moe_ffn(hl, router_w[i], router_bias[i], w_gate[i], w_up[i], w_down[i],
                                          ws_gate[i], ws_up[i], ws_down[i]), g_post_ffn[i])
        if ctx_out:
            xc = xc + g1_c * rms_norm(mc, g_post_mix[i])
            hc = modulate(rms_norm(xc, g_pre_ffn[i]), sh2_c, sc2_c)
            xc = xc + g2_c * rms_norm(moe_ffn(hc, router_w[i], router_bias[i], w_gate[i], w_up[i], w_down[i],
                                              ws_gate[i], ws_up[i], ws_down[i]), g_post_ffn[i])
    return xl
```

```python
import functools
import math

import jax
import jax.numpy as jnp
from jax import lax
from jax.experimental import pallas as pl
from jax.experimental.pallas import tpu as pltpu

F32 = jnp.float32
BF16 = jnp.bfloat16

GRID_W = 64
HEADS = 8
KV_HEADS = 2
HEAD_DIM = 64
Q_PER_KV = HEADS // KV_HEADS
ATT_W = HEADS * HEAD_DIM
ROPE_THETA = 10000.0
SSD_HEADS = 8
SSD_P = 64
SSD_W = SSD_HEADS * SSD_P
SSD_G = 2
SSD_N = 128
CHUNK = 128
CONV_W = SSD_W + 2 * SSD_G * SSD_N
IN_W = ATT_W + 2 * KV_HEADS * HEAD_DIM + SSD_W + CONV_W + 2 * SSD_HEADS
LANES = 128
IN_PAD = IN_W - 2 * SSD_HEADS + LANES
N_EXP = 64
TOP_K = 8
N_GROUPS = 8
TOPK_GROUPS = 4
GROUP_SIZE = N_EXP // N_GROUPS
ROUTED_SCALE = 2.5
EPS = 1e-6
MOD_ROWS = 16
ROW_TILE = 256
VMEM_LIMIT = 56 * 1024 * 1024
NEG_BIG = -1e30

_NT = (((1,), (1,)), ((), ()))


def _params(*sem):
    return pltpu.CompilerParams(dimension_semantics=sem, vmem_limit_bytes=VMEM_LIMIT)


def _sigmoid(v):
    return 1.0 / (1.0 + jnp.exp(-v))


def _silu(v):
    return v * _sigmoid(v)


def _rms(v, gain):
    return v * lax.rsqrt(jnp.mean(v * v, axis=-1, keepdims=True) + EPS) * gain


def _dot(a, b):
    return jnp.dot(a, b, preferred_element_type=F32)


def _split2(v):
    hi = v.astype(BF16)
    lo = (v - hi.astype(F32)).astype(BF16)
    return hi, lo


def _split3(v):
    hi = v.astype(BF16)
    r = v - hi.astype(F32)
    mid = r.astype(BF16)
    lo = (r - mid.astype(F32)).astype(BF16)
    return hi, mid, lo


def _mod_kernel(c_ref, w_ref, b_ref, o_ref):
    s = _silu(c_ref[...])
    o_ref[0] = jnp.dot(s, w_ref[0], preferred_element_type=F32, precision=lax.Precision.HIGHEST) + b_ref[0]


def _modulation(cvec, w_mod, b_mod):
    depth, d, six_d = w_mod.shape
    tn = six_d // 4
    return pl.pallas_call(
        _mod_kernel,
        grid=(depth, six_d // tn),
        in_specs=[pl.BlockSpec((MOD_ROWS, d), lambda l, j: (0, 0)),
                  pl.BlockSpec((1, d, tn), lambda l, j: (l, 0, j)),
                  pl.BlockSpec((1, 1, tn), lambda l, j: (l, 0, j))],
        out_specs=pl.BlockSpec((1, MOD_ROWS, tn), lambda l, j: (l, 0, j)),
        out_shape=jax.ShapeDtypeStruct((depth, MOD_ROWS, six_d), F32),
        compiler_params=_params("arbitrary", "arbitrary"),
        name="modulation",
    )(cvec, w_mod, b_mod.reshape(depth, 1, six_d))


def _head_norm_rope(xb, gain, cos, sin):
    lane = lax.broadcasted_iota(jnp.int32, xb.shape, 1)
    low = lane < HEAD_DIM
    sq = xb * xb
    s_lo = jnp.sum(jnp.where(low, sq, 0.0), axis=-1, keepdims=True)
    s_hi = jnp.sum(jnp.where(low, 0.0, sq), axis=-1, keepdims=True)
    ms = jnp.where(low, s_lo, s_hi) * (1.0 / HEAD_DIM)
    y = xb * lax.rsqrt(ms + EPS) * gain
    ahead = pltpu.roll(y, LANES - HEAD_DIM // 2, 1)
    behind = pltpu.roll(y, HEAD_DIM // 2, 1)
    first_half = (lane % HEAD_DIM) < (HEAD_DIM // 2)
    return y * cos + jnp.where(first_half, ahead, behind) * sin


def _inproj_kernel(x_ref, mod_ref, g_ref, w_ref, qg_ref, kg_ref, cos_ref, sin_ref,
                   q_ref, k_ref, v_ref, z_ref, xbc_ref, dt_ref):
    h = _rms(x_ref[...], g_ref[...]) * (1.0 + mod_ref[0, 1:2, :]) + mod_ref[0, 0:1, :]
    hb = h.astype(BF16)
    cos = cos_ref[...]
    sin = sin_ref[...]
    c0 = 0
    for blk in range(ATT_W // LANES):
        qb = _dot(hb, w_ref[:, c0:c0 + LANES])
        qb = _head_norm_rope(qb, qg_ref[...], cos, sin) * (HEAD_DIM ** -0.5)
        q_ref[:, c0:c0 + LANES] = qb.astype(q_ref.dtype)
        c0 += LANES
    kb = _head_norm_rope(_dot(hb, w_ref[:, c0:c0 + LANES]), kg_ref[...], cos, sin)
    c0 += LANES
    vb = _dot(hb, w_ref[:, c0:c0 + LANES])
    c0 += LANES
    for g in range(KV_HEADS):
        k_ref[g] = kb[:, g * HEAD_DIM:(g + 1) * HEAD_DIM].astype(k_ref.dtype)
        v_ref[g] = vb[:, g * HEAD_DIM:(g + 1) * HEAD_DIM].astype(v_ref.dtype)
    z_ref[...] = _dot(hb, w_ref[:, c0:c0 + SSD_W])
    c0 += SSD_W
    xbc_ref[...] = _dot(hb, w_ref[:, c0:c0 + CONV_W])
    c0 += CONV_W
    dt_ref[...] = _dot(hb, w_ref[:, c0:c0 + LANES])


def _mod_row(start_row, n_lat, seq, n_batch):
    return jnp.where(start_row < n_lat, start_row // seq, n_batch)


def _inproj(xs, mod, g_pre, w_in, q_gain, k_gain, cos_t, sin_t, n_batch, seq, ctx_len):
    n, d = xs.shape
    n_lat = n_batch * seq
    tm = ROW_TILE
    lat_tiles = seq // tm
    ctx_tiles = ctx_len // tm

    def tab_idx(i):
        return (jnp.where(i * tm < n_lat, i % lat_tiles, lat_tiles + (i - n_lat // tm) % ctx_tiles), 0)

    row = lambda i: (i, 0)
    const = lambda i: (0, 0)
    return pl.pallas_call(
        _inproj_kernel,
        grid=(n // tm,),
        in_specs=[pl.BlockSpec((tm, d), row),
                  pl.BlockSpec((1, 6, d), lambda i: (_mod_row(i * tm, n_lat, seq, n_batch), 0, 0)),
                  pl.BlockSpec((1, d), const),
                  pl.BlockSpec((d, IN_PAD), const),
                  pl.BlockSpec((1, LANES), const),
                  pl.BlockSpec((1, LANES), const),
                  pl.BlockSpec((tm, LANES), tab_idx),
                  pl.BlockSpec((tm, LANES), tab_idx)],
        out_specs=[pl.BlockSpec((tm, ATT_W), row),
                   pl.BlockSpec((KV_HEADS, tm, HEAD_DIM), lambda i: (0, i, 0)),
                   pl.BlockSpec((KV_HEADS, tm, HEAD_DIM), lambda i: (0, i, 0)),
                   pl.BlockSpec((tm, SSD_W), row),
                   pl.BlockSpec((tm, CONV_W), row),
                   pl.BlockSpec((tm, LANES), row)],
        out_shape=[jax.ShapeDtypeStruct((n, ATT_W), BF16),
                   jax.ShapeDtypeStruct((KV_HEADS, n, HEAD_DIM), BF16),
                   jax.ShapeDtypeStruct((KV_HEADS, n, HEAD_DIM), BF16),
                   jax.ShapeDtypeStruct((n, SSD_W), F32),
                   jax.ShapeDtypeStruct((n, CONV_W), F32),
                   jax.ShapeDtypeStruct((n, LANES), F32)],
        compiler_params=_params("arbitrary"),
        name="inproj",
    )(xs, mod, g_pre, w_in, q_gain, k_gain, cos_t, sin_t)


def _attn_heads(q_ref, kc_ref, vc_ref, kl_ref, vl_ref, o_ref, latent):
    outs = []
    q = q_ref[...]
    for h in range(HEADS):
        g = h // Q_PER_KV
        qh = q[:, h * HEAD_DIM:(h + 1) * HEAD_DIM]
        sc = lax.dot_general(qh, kc_ref[g], _NT, preferred_element_type=F32)
        m = jnp.max(sc, axis=-1, keepdims=True)
        if latent:
            sl = lax.dot_general(qh, kl_ref[g], _NT, preferred_element_type=F32)
            m = jnp.maximum(m, jnp.max(sl, axis=-1, keepdims=True))
            pw = jnp.exp(sl - m)
        pc = jnp.exp(sc - m)
        den = jnp.sum(pc, axis=-1, keepdims=True)
        acc = _dot(pc.astype(BF16), vc_ref[g])
        if latent:
            den = den + jnp.sum(pw, axis=-1, keepdims=True)
            acc = acc + _dot(pw.astype(BF16), vl_ref[g])
        outs.append(acc / den)
    o_ref[...] = jnp.concatenate(outs, axis=-1).astype(o_ref.dtype)


def _attn_kernel(n_lat_q, with_ctx, q_ref, kc_ref, vc_ref, kl_ref, vl_ref, o_ref):
    if not with_ctx:
        _attn_heads(q_ref, kc_ref, vc_ref, kl_ref, vl_ref, o_ref, True)
        return
    j = pl.program_id(1)

    @pl.when(j < n_lat_q)
    def _():
        _attn_heads(q_ref, kc_ref, vc_ref, kl_ref, vl_ref, o_ref, True)

    @pl.when(j >= n_lat_q)
    def _():
        _attn_heads(q_ref, kc_ref, vc_ref, kl_ref, vl_ref, o_ref, False)


def _attention(q, k, v, n_batch, seq, ctx_len, with_ctx):
    n = q.shape[0]
    n_lat = n_batch * seq
    tq = ROW_TILE
    n_lat_q = seq // tq
    n_ctx_q = ctx_len // tq if with_ctx else 0

    def q_idx(b, j):
        return (jnp.where(j < n_lat_q, b * n_lat_q + j, n_lat // tq + b * (ctx_len // tq) + (j - n_lat_q)), 0)

    ctx_idx = lambda b, j: (0, n_lat // ctx_len + b, 0)
    lat_idx = lambda b, j: (0, b, 0)
    return pl.pallas_call(
        functools.partial(_attn_kernel, n_lat_q, with_ctx),
        grid=(n_batch, n_lat_q + n_ctx_q),
        in_specs=[pl.BlockSpec((tq, ATT_W), q_idx),
                  pl.BlockSpec((KV_HEADS, ctx_len, HEAD_DIM), ctx_idx),
                  pl.BlockSpec((KV_HEADS, ctx_len, HEAD_DIM), ctx_idx),
                  pl.BlockSpec((KV_HEADS, seq, HEAD_DIM), lat_idx),
                  pl.BlockSpec((KV_HEADS, seq, HEAD_DIM), lat_idx)],
        out_specs=pl.BlockSpec((tq, ATT_W), q_idx),
        out_shape=jax.ShapeDtypeStruct((n, ATT_W), BF16),
        compiler_params=_params("arbitrary", "arbitrary"),
        name="attention",
    )(q, k, v, k, v)


def _conv_kernel(lat_tiles, ctx_tiles, n_lat_tiles, x_ref, p_ref, nx_ref, w_ref, b_ref, o_ref):
    i = pl.program_id(0)
    is_lat = i < n_lat_tiles
    pos = jnp.where(is_lat, i % lat_tiles, (i - n_lat_tiles) % ctx_tiles)
    last_pos = jnp.where(is_lat, lat_tiles - 1, ctx_tiles - 1)
    u = x_ref[...]
    rows = u.shape[0]
    r = lax.broadcasted_iota(jnp.int32, u.shape, 0)
    prev_row = jnp.where(pos == 0, 0.0, p_ref[7:8, :])
    next_row = jnp.where(pos == last_pos, 0.0, nx_ref[0:1, :])
    up = jnp.where(r == 0, prev_row, pltpu.roll(u, 1, 0))
    dn = jnp.where(r == rows - 1, next_row, pltpu.roll(u, rows - 1, 0))
    y = w_ref[0:1, :] * up + w_ref[1:2, :] * u + w_ref[2:3, :] * dn + b_ref[...]
    o_ref[...] = _silu(y)


def _conv(xbc, conv_w, conv_b, n_batch, seq, ctx_len):
    n, cw = xbc.shape
    tm = ROW_TILE
    sub = 8
    per = tm // sub
    return pl.pallas_call(
        functools.partial(_conv_kernel, seq // tm, ctx_len // tm, n_batch * seq // tm),
        grid=(n // tm,),
        in_specs=[pl.BlockSpec((tm, cw), lambda i: (i, 0)),
                  pl.BlockSpec((sub, cw), lambda i: (jnp.maximum(i * per - 1, 0), 0)),
                  pl.BlockSpec((sub, cw), lambda i: (jnp.minimum((i + 1) * per, n // sub - 1), 0)),
                  pl.BlockSpec((3, cw), lambda i: (0, 0)),
                  pl.BlockSpec((1, cw), lambda i: (0, 0))],
        out_specs=pl.BlockSpec((tm, cw), lambda i: (i, 0)),
        out_shape=jax.ShapeDtypeStruct((n, cw), F32),
        compiler_params=_params("arbitrary"),
        name="conv",
    )(xbc, xbc, xbc, conv_w, conv_b)


def _ssd_kernel(direction, *refs):
    if direction == 0:
        xs_ref, b_ref, c_ref, dt_ref, dtb_ref, a_ref, ex_ref, y_ref, st_ref = refs
    else:
        (xs_ref, b_ref, c_ref, dt_ref, dtb_ref, a_ref, ex_ref, yf_ref, z_ref, dsk_ref, gn_ref,
         y_ref, st_ref) = refs

    @pl.when(pl.program_id(1) == 0)
    def _():
        st_ref[...] = jnp.zeros_like(st_ref)

    xs = xs_ref[...]
    pre = dt_ref[...] + dtb_ref[...]
    dtv = jnp.maximum(pre, 0.0) + jnp.log1p(jnp.exp(-jnp.abs(pre)))
    la = dtv * a_ref[...]
    row = lax.broadcasted_iota(jnp.int32, (CHUNK, CHUNK), 0)
    col = lax.broadcasted_iota(jnp.int32, (CHUNK, CHUNK), 1)
    tri = (col <= row) if direction == 0 else (col >= row)
    tri_b = jnp.where(tri, 1.0, 0.0).astype(BF16)
    cs = sum(_dot(tri_b, part) for part in _split3(la))
    ex = ex_ref[...]

    def expand(val):
        return sum(_dot(part, ex) for part in _split2(val))

    edge = CHUNK - 1 if direction == 0 else 0
    tot = cs[edge:edge + 1, :]
    dt_e = expand(dtv)
    da_e = expand(jnp.exp(cs))
    db_e = expand(jnp.exp(tot - cs))
    dtot_e = da_e[edge:edge + 1, :]
    xd = xs * dt_e
    xd_end = (xd * db_e).astype(BF16)
    cs_t = cs.T
    st = st_ref[...]
    st_b = st.astype(BF16)
    ys = []
    new_st = []
    gw = SSD_W // SSD_G
    hpg = SSD_HEADS // SSD_G
    for g in range(SSD_G):
        bg = b_ref[:, g * SSD_N:(g + 1) * SSD_N]
        cg = c_ref[:, g * SSD_N:(g + 1) * SSD_N].astype(BF16)
        gram = lax.dot_general(cg, bg.astype(BF16), _NT, preferred_element_type=F32)
        y_off = _dot(cg, st_b[:, g * gw:(g + 1) * gw]) * da_e[:, g * gw:(g + 1) * gw]
        new_st.append(st[:, g * gw:(g + 1) * gw] * dtot_e[:, g * gw:(g + 1) * gw]
                      + _dot(bg.T.astype(BF16), xd_end[:, g * gw:(g + 1) * gw]))
        for hh in range(hpg):
            h = g * hpg + hh
            c = direction * SSD_HEADS + h
            diff = cs[:, c:c + 1] - cs_t[c:c + 1, :]
            decay = jnp.exp(jnp.where(tri, diff, NEG_BIG))
            m = (gram * decay).astype(BF16)
            y_d = _dot(m, xd[:, h * SSD_P:(h + 1) * SSD_P].astype(BF16))
            ys.append(y_d + y_off[:, hh * SSD_P:(hh + 1) * SSD_P])
    st_ref[...] = jnp.concatenate(new_st, axis=-1)
    y = jnp.concatenate(ys, axis=-1)
    if direction == 0:
        y_ref[...] = y
    else:
        z = z_ref[...]
        gated = (yf_ref[...] + y + xs * dsk_ref[...]) * _silu(z)
        y_ref[...] = _rms(gated, gn_ref[...]).astype(y_ref.dtype)


def _ssd(direction, xc, dt, dtb, a_neg, ex, n_batch, seq, ctx_len, extra=None):
    n = xc.shape[0]
    n_lat = n_batch * seq
    ncc = ctx_len // CHUNK
    nlc = seq // CHUNK

    def rows(b, j):
        if direction == 0:
            return jnp.where(j < ncc, n_lat // CHUNK + b * ncc + j, b * nlc + (j - ncc))
        return jnp.where(j < ncc, n_lat // CHUNK + b * ncc + (ncc - 1 - j), b * nlc + (nlc - 1 - (j - ncc)))

    const = lambda b, j: (0, 0)
    in_specs = [pl.BlockSpec((CHUNK, SSD_W), lambda b, j: (rows(b, j), 0)),
                pl.BlockSpec((CHUNK, SSD_G * SSD_N), lambda b, j: (rows(b, j), 2)),
                pl.BlockSpec((CHUNK, SSD_G * SSD_N), lambda b, j: (rows(b, j), 3)),
                pl.BlockSpec((CHUNK, LANES), lambda b, j: (rows(b, j), 0)),
                pl.BlockSpec((1, LANES), const),
                pl.BlockSpec((1, LANES), const),
                pl.BlockSpec((LANES, SSD_W), const)]
    args = [xc, xc, xc, dt, dtb, a_neg, ex]
    if direction == 1:
        yf, z, dskip, gnorm = extra
        in_specs += [pl.BlockSpec((CHUNK, SSD_W), lambda b, j: (rows(b, j), 0)),
                     pl.BlockSpec((CHUNK, SSD_W), lambda b, j: (rows(b, j), 0)),
                     pl.BlockSpec((1, SSD_W), const),
                     pl.BlockSpec((1, SSD_W), const)]
        args += [yf, z, dskip, gnorm]
    return pl.pallas_call(
        functools.partial(_ssd_kernel, direction),
        grid=(n_batch, ncc + nlc),
        in_specs=in_specs,
        out_specs=pl.BlockSpec((CHUNK, SSD_W), lambda b, j: (rows(b, j), 0)),
        out_shape=jax.ShapeDtypeStruct((n, SSD_W), F32 if direction == 0 else BF16),
        scratch_shapes=[pltpu.VMEM((SSD_N, SSD_W), F32)],
        compiler_params=_params("arbitrary", "arbitrary"),
        name="ssd_fwd" if direction == 0 else "ssd_bwd",
    )(*args)


def _route(logits, bias):
    tokens = logits.shape[1]
    scores = _sigmoid(logits)
    choice = scores + bias
    sub = lax.broadcasted_iota(jnp.int32, (GROUP_SIZE, tokens), 0).astype(F32)
    neg = -jnp.inf

    def pick_first_max(v, idx, sentinel):
        m = jnp.max(v, axis=0, keepdims=True)
        first = jnp.min(jnp.where(v == m, idx, sentinel), axis=0, keepdims=True)
        return m, idx == first

    group_scores = []
    for g in range(N_GROUPS):
        cg = choice[g * GROUP_SIZE:(g + 1) * GROUP_SIZE, :]
        m1, hit = pick_first_max(cg, sub, GROUP_SIZE)
        m2 = jnp.max(jnp.where(hit, neg, cg), axis=0, keepdims=True)
        group_scores.append(m1 + m2)
    v = jnp.concatenate(group_scores, axis=0)
    gsel = jnp.zeros_like(v)
    for _ in range(TOPK_GROUPS):
        _, hit = pick_first_max(v, sub, N_GROUPS)
        gsel = jnp.where(hit, 1.0, gsel)
        v = jnp.where(hit, neg, v)
    ok = jnp.concatenate([jnp.broadcast_to(gsel[g:g + 1, :], (GROUP_SIZE, tokens)) for g in range(N_GROUPS)],
                         axis=0) > 0.5
    v = jnp.where(ok, choice, neg)
    eidx = lax.broadcasted_iota(jnp.int32, (N_EXP, tokens), 0).astype(F32)
    sel = jnp.zeros_like(v)
    for _ in range(TOP_K):
        _, hit = pick_first_max(v, eidx, N_EXP)
        sel = jnp.where(hit, 1.0, sel)
        v = jnp.where(hit, neg, v)
    gate = jnp.where(sel > 0.5, scores, 0.0)
    return gate / jnp.sum(gate, axis=0, keepdims=True) * ROUTED_SCALE


def _outproj_kernel(att_ref, ssd_ref, x_ref, wo_ref, mod_ref, gpost_ref, gpre_ref, wr_ref, rb_ref,
                    x1_ref, h2_ref, comb_ref):
    m = _dot(att_ref[...], wo_ref[0:ATT_W, :]) + _dot(ssd_ref[...], wo_ref[ATT_W:, :])
    x1 = x_ref[...] + mod_ref[0, 2:3, :] * _rms(m, gpost_ref[...])
    x1_ref[...] = x1
    h2 = _rms(x1, gpre_ref[...]) * (1.0 + mod_ref[0, 4:5, :]) + mod_ref[0, 3:4, :]
    h2_ref[...] = h2.astype(h2_ref.dtype)
    logits = lax.dot_general(wr_ref[...], h2, _NT, preferred_element_type=F32,
                             precision=lax.Precision.HIGHEST)
    comb_t = _route(logits, rb_ref[...])
    pad = jnp.zeros((LANES - N_EXP, comb_t.shape[1]), F32)
    comb_ref[...] = jnp.concatenate([comb_t, pad], axis=0).T


def _outproj(att, ssd, xs, w_out, mod, g_post, g_pre, wr_t, r_bias, n_rows, n_batch, seq):
    d = xs.shape[1]
    n_lat = n_batch * seq
    tm = ROW_TILE
    row = lambda i: (i, 0)
    const = lambda i: (0, 0)
    return pl.pallas_call(
        _outproj_kernel,
        grid=(n_rows // tm,),
        in_specs=[pl.BlockSpec((tm, ATT_W), row),
                  pl.BlockSpec((tm, SSD_W), row),
                  pl.BlockSpec((tm, d), row),
                  pl.BlockSpec((ATT_W + SSD_W, d), const),
                  pl.BlockSpec((1, 6, d), lambda i: (_mod_row(i * tm, n_lat, seq, n_batch), 0, 0)),
                  pl.BlockSpec((1, d), const),
                  pl.BlockSpec((1, d), const),
                  pl.BlockSpec((N_EXP, d), const),
                  pl.BlockSpec((N_EXP, 1), const)],
        out_specs=[pl.BlockSpec((tm, d), row),
                   pl.BlockSpec((tm, d), row),
                   pl.BlockSpec((tm, LANES), row)],
        out_shape=[jax.ShapeDtypeStruct((n_rows, d), F32),
                   jax.ShapeDtypeStruct((n_rows, d), BF16),
                   jax.ShapeDtypeStruct((n_rows, LANES), F32)],
        compiler_params=_params("arbitrary"),
        name="outproj_router",
    )(att, ssd, xs, w_out, mod, g_post, g_pre, wr_t, r_bias)


def _moe_kernel(h_ref, comb_ref, x1_ref, mod_ref, gpost_ref, wg_ref, wu_ref, wd_ref, sg_ref, su_ref, sd_ref,
                o_ref, acc_ref):
    e = pl.program_id(1)
    h = h_ref[...]

    @pl.when(e == 0)
    def _():
        a = _silu(_dot(h, sg_ref[...])) * _dot(h, su_ref[...])
        acc_ref[...] = _dot(a.astype(BF16), sd_ref[...])

    comb = comb_ref[...]
    lane = lax.broadcasted_iota(jnp.int32, comb.shape, 1)
    gate = jnp.sum(jnp.where(lane == e, comb, 0.0), axis=-1, keepdims=True)
    a = _silu(_dot(h, wg_ref[0])) * _dot(h, wu_ref[0]) * gate
    acc_ref[...] += _dot(a.astype(BF16), wd_ref[0])

    @pl.when(e == pl.num_programs(1) - 1)
    def _():
        o_ref[...] = x1_ref[...] + mod_ref[0, 5:6, :] * _rms(acc_ref[...], gpost_ref[...])


def _moe(h2, comb, x1, mod, g_post, w_gate, w_up, w_down, ws_gate, ws_up, ws_down, n_batch, seq):
    n_rows, d = x1.shape
    n_lat = n_batch * seq
    tm = min(1024, seq)
    fe = w_gate.shape[-1]
    fs = ws_gate.shape[-1]
    row = lambda i, e: (i, 0)
    const = lambda i, e: (0, 0)
    return pl.pallas_call(
        _moe_kernel,
        grid=(n_rows // tm, N_EXP),
        in_specs=[pl.BlockSpec((tm, d), row),
                  pl.BlockSpec((tm, LANES), row),
                  pl.BlockSpec((tm, d), row),
                  pl.BlockSpec((1, 6, d), lambda i, e: (_mod_row(i * tm, n_lat, seq, n_batch), 0, 0)),
                  pl.BlockSpec((1, d), const),
                  pl.BlockSpec((1, d, fe), lambda i, e: (e, 0, 0)),
                  pl.BlockSpec((1, d, fe), lambda i, e: (e, 0, 0)),
                  pl.BlockSpec((1, fe, d), lambda i, e: (e, 0, 0)),
                  pl.BlockSpec((d, fs), const),
                  pl.BlockSpec((d, fs), const),
                  pl.BlockSpec((fs, d), const)],
        out_specs=pl.BlockSpec((tm, d), row),
        out_shape=jax.ShapeDtypeStruct((n_rows, d), F32),
        scratch_shapes=[pltpu.VMEM((tm, d), F32)],
        compiler_params=_params("arbitrary", "arbitrary"),
        name="moe",
    )(h2, comb, x1, mod, g_post, w_gate, w_up, w_down, ws_gate, ws_up, ws_down)


def _rope_tables(seq, ctx_len):
    rows = seq // GRID_W
    row = jnp.repeat(jnp.arange(rows), GRID_W).astype(F32)
    col = jnp.tile(jnp.arange(GRID_W), rows).astype(F32)
    n_freq = HEAD_DIM // 4
    inv_freq = ROPE_THETA ** (-jnp.arange(n_freq, dtype=F32) / n_freq)
    ang = jnp.concatenate([row[:, None] * inv_freq, col[:, None] * inv_freq], axis=-1)
    reps = LANES // (HEAD_DIM // 2)
    cos = jnp.tile(jnp.cos(ang), (1, reps))
    sign = jnp.where((jnp.arange(LANES) % HEAD_DIM) < HEAD_DIM // 2, -1.0, 1.0).astype(F32)
    sin = jnp.tile(jnp.sin(ang), (1, reps)) * sign
    cos = jnp.concatenate([cos, jnp.ones((ctx_len, LANES), F32)], axis=0)
    sin = jnp.concatenate([sin, jnp.zeros((ctx_len, LANES), F32)], axis=0)
    return cos, sin


def _pad_lanes(v):
    return jnp.pad(v.reshape(1, -1), ((0, 0), (0, LANES - v.size)))


def kernel(x, c, ctx, c_ctx, w_mod, b_mod, g_pre_mix, g_post_mix, g_pre_ffn, g_post_ffn, w_in, q_norm, k_norm,
           conv_w, conv_b, dt_bias, a_log, d_skip, ssd_norm, w_out, router_w, router_bias, w_gate, w_up, w_down,
           ws_gate, ws_up, ws_down):
    n_batch, seq, d = x.shape
    ctx_len = ctx.shape[1]
    depth = w_mod.shape[0]
    n_lat = n_batch * seq
    assert n_batch < MOD_ROWS and seq % ROW_TILE == 0 and ctx_len % ROW_TILE == 0 and seq % GRID_W == 0

    xs = jnp.concatenate([x.reshape(n_lat, d), ctx.reshape(n_batch * ctx_len, d)], axis=0)
    cvec = jnp.zeros((MOD_ROWS, d), F32).at[:n_batch].set(c).at[n_batch].set(c_ctx)
    mod_all = _modulation(cvec, w_mod, b_mod)
    cos_t, sin_t = _rope_tables(seq, ctx_len)
    head_lane = jnp.arange(LANES)[:, None]
    chan_head = jnp.arange(SSD_W)[None, :] // SSD_P

    for i in range(depth):
        last = i == depth - 1
        mod = mod_all[i].reshape(MOD_ROWS, 6, d)
        w_in_b = jnp.pad(w_in[i].astype(BF16), ((0, 0), (0, IN_PAD - IN_W)))
        q, k, v, z, xbc, dt = _inproj(xs, mod, g_pre_mix[i].reshape(1, d), w_in_b,
                                      jnp.tile(q_norm[i], LANES // HEAD_DIM).reshape(1, LANES),
                                      jnp.tile(k_norm[i], LANES // HEAD_DIM).reshape(1, LANES),
                                      cos_t, sin_t, n_batch, seq, ctx_len)
        att = _attention(q, k, v, n_batch, seq, ctx_len, with_ctx=not last)
        xc = _conv(xbc, conv_w[i], conv_b[i].reshape(1, CONV_W), n_batch, seq, ctx_len)
        dtb = _pad_lanes(dt_bias[i])
        a_neg = _pad_lanes(-jnp.exp(a_log[i]))
        y_f = _ssd(0, xc, dt, dtb, a_neg, (head_lane == chan_head).astype(BF16), n_batch, seq, ctx_len)
        extra = (y_f, z, jnp.repeat(d_skip[i], SSD_P).reshape(1, SSD_W), ssd_norm[i].reshape(1, SSD_W))
        ssd = _ssd(1, xc, dt, dtb, a_neg, (head_lane == chan_head + SSD_HEADS).astype(BF16),
                   n_batch, seq, ctx_len, extra)
        n_rows = n_lat if last else xs.shape[0]
        x1, h2, comb = _outproj(att, ssd, xs, w_out[i].astype(BF16), mod, g_post_mix[i].reshape(1, d),
                                g_pre_ffn[i].reshape(1, d), router_w[i].T, router_bias[i].reshape(N_EXP, 1),
                                n_rows, n_batch, seq)
        xs = _moe(h2, comb, x1, mod, g_post_ffn[i].reshape(1, d), w_gate[i].astype(BF16), w_up[i].astype(BF16),
                  w_down[i].astype(BF16), ws_gate[i].astype(BF16), ws_up[i].astype(BF16),
                  ws_down[i].astype(BF16), n_batch, seq)
    return xs.reshape(n_batch, seq, d)
```

```python
import functools
import math

import jax
import jax.numpy as jnp
from jax import lax
from jax.experimental import pallas as pl
from jax.experimental.pallas import tpu as pltpu

F32 = jnp.float32
BF16 = jnp.bfloat16

GRID_W = 64
HEADS = 8
KV_HEADS = 2
HEAD_DIM = 64
Q_PER_KV = HEADS // KV_HEADS
ATT_W = HEADS * HEAD_DIM
ROPE_THETA = 10000.0
SSD_HEADS = 8
SSD_P = 64
SSD_W = SSD_HEADS * SSD_P
SSD_G = 2
SSD_N = 128
CHUNK = 128
CONV_W = SSD_W + 2 * SSD_G * SSD_N
IN_W = ATT_W + 2 * KV_HEADS * HEAD_DIM + SSD_W + CONV_W + 2 * SSD_HEADS
LANES = 128
IN_PAD = IN_W - 2 * SSD_HEADS + LANES
N_EXP = 64
TOP_K = 8
N_GROUPS = 8
TOPK_GROUPS = 4
GROUP_SIZE = N_EXP // N_GROUPS
ROUTED_SCALE = 2.5
EPS = 1e-6
MOD_ROWS = 16
ROW_TILE = 256
MAX_WINDOW = 2048
GROUP = 128
UNROLL = 8
SUB = 8
VMEM_LIMIT = 56 * 1024 * 1024
NEG_BIG = -1e30

_NT = (((1,), (1,)), ((), ()))


def _params(*sem):
    return pltpu.CompilerParams(dimension_semantics=sem, vmem_limit_bytes=VMEM_LIMIT)


def _sigmoid(v):
    return 1.0 / (1.0 + jnp.exp(-v))


def _silu(v):
    return v * _sigmoid(v)


def _rms(v, gain):
    return v * lax.rsqrt(jnp.mean(v * v, axis=-1, keepdims=True) + EPS) * gain


def _dot(a, b):
    return jnp.dot(a, b, preferred_element_type=F32)


def _split2(v):
    hi = v.astype(BF16)
    lo = (v - hi.astype(F32)).astype(BF16)
    return hi, lo


def _split3(v):
    hi = v.astype(BF16)
    r = v - hi.astype(F32)
    mid = r.astype(BF16)
    lo = (r - mid.astype(F32)).astype(BF16)
    return hi, mid, lo


def _mod_kernel(c_ref, w_ref, b_ref, o_ref):
    s = _silu(c_ref[...])
    o_ref[0] = jnp.dot(s, w_ref[0], preferred_element_type=F32, precision=lax.Precision.HIGHEST) + b_ref[0]


def _modulation(cvec, w_mod, b_mod):
    depth, d, six_d = w_mod.shape
    tn = six_d // 4
    return pl.pallas_call(
        _mod_kernel,
        grid=(depth, six_d // tn),
        in_specs=[pl.BlockSpec((MOD_ROWS, d), lambda l, j: (0, 0)),
                  pl.BlockSpec((1, d, tn), lambda l, j: (l, 0, j)),
                  pl.BlockSpec((1, 1, tn), lambda l, j: (l, 0, j))],
        out_specs=pl.BlockSpec((1, MOD_ROWS, tn), lambda l, j: (l, 0, j)),
        out_shape=jax.ShapeDtypeStruct((depth, MOD_ROWS, six_d), F32),
        compiler_params=_params("arbitrary", "arbitrary"),
        name="modulation",
    )(cvec, w_mod, b_mod.reshape(depth, 1, six_d))


def _head_norm_rope(xb, gain, cos, sin):
    lane = lax.broadcasted_iota(jnp.int32, xb.shape, 1)
    low = lane < HEAD_DIM
    sq = xb * xb
    s_lo = jnp.sum(jnp.where(low, sq, 0.0), axis=-1, keepdims=True)
    s_hi = jnp.sum(jnp.where(low, 0.0, sq), axis=-1, keepdims=True)
    ms = jnp.where(low, s_lo, s_hi) * (1.0 / HEAD_DIM)
    y = xb * lax.rsqrt(ms + EPS) * gain
    ahead = pltpu.roll(y, LANES - HEAD_DIM // 2, 1)
    behind = pltpu.roll(y, HEAD_DIM // 2, 1)
    first_half = (lane % HEAD_DIM) < (HEAD_DIM // 2)
    return y * cos + jnp.where(first_half, ahead, behind) * sin


def _inproj_kernel(x_ref, mod_ref, g_ref, w_ref, qg_ref, kg_ref, cos_ref, sin_ref,
                   q_ref, k_ref, v_ref, z_ref, xbc_ref, dt_ref):
    h = _rms(x_ref[...], g_ref[...]) * (1.0 + mod_ref[0, 1:2, :]) + mod_ref[0, 0:1, :]
    hb = h.astype(BF16)
    cos = cos_ref[...]
    sin = sin_ref[...]
    c0 = 0
    for blk in range(ATT_W // LANES):
        qb = _dot(hb, w_ref[:, c0:c0 + LANES])
        qb = _head_norm_rope(qb, qg_ref[...], cos, sin) * (HEAD_DIM ** -0.5)
        q_ref[:, c0:c0 + LANES] = qb.astype(q_ref.dtype)
        c0 += LANES
    kb = _head_norm_rope(_dot(hb, w_ref[:, c0:c0 + LANES]), kg_ref[...], cos, sin)
    c0 += LANES
    vb = _dot(hb, w_ref[:, c0:c0 + LANES])
    c0 += LANES
    for g in range(KV_HEADS):
        k_ref[g] = kb[:, g * HEAD_DIM:(g + 1) * HEAD_DIM].astype(k_ref.dtype)
        v_ref[g] = vb[:, g * HEAD_DIM:(g + 1) * HEAD_DIM].astype(v_ref.dtype)
    z_ref[...] = _dot(hb, w_ref[:, c0:c0 + SSD_W])
    c0 += SSD_W
    xbc_ref[...] = _dot(hb, w_ref[:, c0:c0 + CONV_W])
    c0 += CONV_W
    dt_ref[...] = _dot(hb, w_ref[:, c0:c0 + LANES])


def _mod_row(start_row, n_lat, seq, n_batch):
    return jnp.where(start_row < n_lat, start_row // seq, n_batch)


def _inproj(xs, mod, g_pre, w_in, q_gain, k_gain, cos_t, sin_t, n_batch, seq, ctx_len):
    n, d = xs.shape
    n_lat = n_batch * seq
    tm = ROW_TILE
    lat_tiles = seq // tm
    ctx_tiles = ctx_len // tm

    def tab_idx(i):
        return (jnp.where(i * tm < n_lat, i % lat_tiles, lat_tiles + (i - n_lat // tm) % ctx_tiles), 0)

    row = lambda i: (i, 0)
    const = lambda i: (0, 0)
    return pl.pallas_call(
        _inproj_kernel,
        grid=(n // tm,),
        in_specs=[pl.BlockSpec((tm, d), row),
                  pl.BlockSpec((1, 6, d), lambda i: (_mod_row(i * tm, n_lat, seq, n_batch), 0, 0)),
                  pl.BlockSpec((1, d), const),
                  pl.BlockSpec((d, IN_PAD), const),
                  pl.BlockSpec((1, LANES), const),
                  pl.BlockSpec((1, LANES), const),
                  pl.BlockSpec((tm, LANES), tab_idx),
                  pl.BlockSpec((tm, LANES), tab_idx)],
        out_specs=[pl.BlockSpec((tm, ATT_W), row),
                   pl.BlockSpec((KV_HEADS, tm, HEAD_DIM), lambda i: (0, i, 0)),
                   pl.BlockSpec((KV_HEADS, tm, HEAD_DIM), lambda i: (0, i, 0)),
                   pl.BlockSpec((tm, SSD_W), row),
                   pl.BlockSpec((tm, CONV_W), row),
                   pl.BlockSpec((tm, LANES), row)],
        out_shape=[jax.ShapeDtypeStruct((n, ATT_W), BF16),
                   jax.ShapeDtypeStruct((KV_HEADS, n, HEAD_DIM), BF16),
                   jax.ShapeDtypeStruct((KV_HEADS, n, HEAD_DIM), BF16),
                   jax.ShapeDtypeStruct((n, SSD_W), F32),
                   jax.ShapeDtypeStruct((n, CONV_W), F32),
                   jax.ShapeDtypeStruct((n, LANES), F32)],
        compiler_params=_params("arbitrary"),
        name="inproj",
    )(xs, mod, g_pre, w_in, q_gain, k_gain, cos_t, sin_t)


def _attn_heads(q_ref, kc_ref, vc_ref, kl_ref, vl_ref, o_ref, latent):
    outs = []
    q = q_ref[...]
    for h in range(HEADS):
        g = h // Q_PER_KV
        qh = q[:, h * HEAD_DIM:(h + 1) * HEAD_DIM]
        sc = lax.dot_general(qh, kc_ref[g], _NT, preferred_element_type=F32)
        m = jnp.max(sc, axis=-1, keepdims=True)
        if latent:
            sl = lax.dot_general(qh, kl_ref[g], _NT, preferred_element_type=F32)
            m = jnp.maximum(m, jnp.max(sl, axis=-1, keepdims=True))
            pw = jnp.exp(sl - m)
        pc = jnp.exp(sc - m)
        den = jnp.sum(pc, axis=-1, keepdims=True)
        acc = _dot(pc.astype(BF16), vc_ref[g])
        if latent:
            den = den + jnp.sum(pw, axis=-1, keepdims=True)
            acc = acc + _dot(pw.astype(BF16), vl_ref[g])
        outs.append(acc / den)
    o_ref[...] = jnp.concatenate(outs, axis=-1).astype(o_ref.dtype)


def _attn_kernel(n_lat_q, with_ctx, q_ref, kc_ref, vc_ref, kl_ref, vl_ref, o_ref):
    if not with_ctx:
        _attn_heads(q_ref, kc_ref, vc_ref, kl_ref, vl_ref, o_ref, True)
        return
    j = pl.program_id(1)

    @pl.when(j < n_lat_q)
    def _():
        _attn_heads(q_ref, kc_ref, vc_ref, kl_ref, vl_ref, o_ref, True)

    @pl.when(j >= n_lat_q)
    def _():
        _attn_heads(q_ref, kc_ref, vc_ref, kl_ref, vl_ref, o_ref, False)


def _attention(q, k, v, n_batch, seq, ctx_len, with_ctx):
    n = q.shape[0]
    n_lat = n_batch * seq
    tq = ROW_TILE
    n_lat_q = seq // tq
    n_ctx_q = ctx_len // tq if with_ctx else 0

    def q_idx(b, j):
        return (jnp.where(j < n_lat_q, b * n_lat_q + j, n_lat // tq + b * (ctx_len // tq) + (j - n_lat_q)), 0)

    ctx_idx = lambda b, j: (0, n_lat // ctx_len + b, 0)
    lat_idx = lambda b, j: (0, b, 0)
    return pl.pallas_call(
        functools.partial(_attn_kernel, n_lat_q, with_ctx),
        grid=(n_batch, n_lat_q + n_ctx_q),
        in_specs=[pl.BlockSpec((tq, ATT_W), q_idx),
                  pl.BlockSpec((KV_HEADS, ctx_len, HEAD_DIM), ctx_idx),
                  pl.BlockSpec((KV_HEADS, ctx_len, HEAD_DIM), ctx_idx),
                  pl.BlockSpec((KV_HEADS, seq, HEAD_DIM), lat_idx),
                  pl.BlockSpec((KV_HEADS, seq, HEAD_DIM), lat_idx)],
        out_specs=pl.BlockSpec((tq, ATT_W), q_idx),
        out_shape=jax.ShapeDtypeStruct((n, ATT_W), BF16),
        compiler_params=_params("arbitrary", "arbitrary"),
        name="attention",
    )(q, k, v, k, v)


def _conv_kernel(lat_tiles, ctx_tiles, n_lat_tiles, x_ref, p_ref, nx_ref, w_ref, b_ref, o_ref):
    i = pl.program_id(0)
    is_lat = i < n_lat_tiles
    pos = jnp.where(is_lat, i % lat_tiles, (i - n_lat_tiles) % ctx_tiles)
    last_pos = jnp.where(is_lat, lat_tiles - 1, ctx_tiles - 1)
    u = x_ref[...]
    rows = u.shape[0]
    r = lax.broadcasted_iota(jnp.int32, u.shape, 0)
    prev_row = jnp.where(pos == 0, 0.0, p_ref[7:8, :])
    next_row = jnp.where(pos == last_pos, 0.0, nx_ref[0:1, :])
    up = jnp.where(r == 0, prev_row, pltpu.roll(u, 1, 0))
    dn = jnp.where(r == rows - 1, next_row, pltpu.roll(u, rows - 1, 0))
    y = w_ref[0:1, :] * up + w_ref[1:2, :] * u + w_ref[2:3, :] * dn + b_ref[...]
    o_ref[...] = _silu(y)


def _conv(xbc, conv_w, conv_b, n_batch, seq, ctx_len):
    n, cw = xbc.shape
    tm = ROW_TILE
    sub = 8
    per = tm // sub
    return pl.pallas_call(
        functools.partial(_conv_kernel, seq // tm, ctx_len // tm, n_batch * seq // tm),
        grid=(n // tm,),
        in_specs=[pl.BlockSpec((tm, cw), lambda i: (i, 0)),
                  pl.BlockSpec((sub, cw), lambda i: (jnp.maximum(i * per - 1, 0), 0)),
                  pl.BlockSpec((sub, cw), lambda i: (jnp.minimum((i + 1) * per, n // sub - 1), 0)),
                  pl.BlockSpec((3, cw), lambda i: (0, 0)),
                  pl.BlockSpec((1, cw), lambda i: (0, 0))],
        out_specs=pl.BlockSpec((tm, cw), lambda i: (i, 0)),
        out_shape=jax.ShapeDtypeStruct((n, cw), F32),
        compiler_params=_params("arbitrary"),
        name="conv",
    )(xbc, xbc, xbc, conv_w, conv_b)


def _ssd_kernel(direction, *refs):
    if direction == 0:
        xs_ref, b_ref, c_ref, dt_ref, dtb_ref, a_ref, ex_ref, y_ref, st_ref = refs
    else:
        (xs_ref, b_ref, c_ref, dt_ref, dtb_ref, a_ref, ex_ref, yf_ref, z_ref, dsk_ref, gn_ref,
         y_ref, st_ref) = refs

    @pl.when(pl.program_id(1) == 0)
    def _():
        st_ref[...] = jnp.zeros_like(st_ref)

    xs = xs_ref[...]
    pre = dt_ref[...] + dtb_ref[...]
    dtv = jnp.maximum(pre, 0.0) + jnp.log1p(jnp.exp(-jnp.abs(pre)))
    la = dtv * a_ref[...]
    row = lax.broadcasted_iota(jnp.int32, (CHUNK, CHUNK), 0)
    col = lax.broadcasted_iota(jnp.int32, (CHUNK, CHUNK), 1)
    tri = (col <= row) if direction == 0 else (col >= row)
    tri_b = jnp.where(tri, 1.0, 0.0).astype(BF16)
    cs = sum(_dot(tri_b, part) for part in _split3(la))
    ex = ex_ref[...]

    def expand(val):
        return sum(_dot(part, ex) for part in _split2(val))

    edge = CHUNK - 1 if direction == 0 else 0
    tot = cs[edge:edge + 1, :]
    dt_e = expand(dtv)
    da_e = expand(jnp.exp(cs))
    db_e = expand(jnp.exp(tot - cs))
    dtot_e = da_e[edge:edge + 1, :]
    xd = xs * dt_e
    xd_end = (xd * db_e).astype(BF16)
    cs_t = cs.T
    st = st_ref[...]
    st_b = st.astype(BF16)
    ys = []
    new_st = []
    gw = SSD_W // SSD_G
    hpg = SSD_HEADS // SSD_G
    for g in range(SSD_G):
        bg = b_ref[:, g * SSD_N:(g + 1) * SSD_N]
        cg = c_ref[:, g * SSD_N:(g + 1) * SSD_N].astype(BF16)
        gram = lax.dot_general(cg, bg.astype(BF16), _NT, preferred_element_type=F32)
        y_off = _dot(cg, st_b[:, g * gw:(g + 1) * gw]) * da_e[:, g * gw:(g + 1) * gw]
        new_st.append(st[:, g * gw:(g + 1) * gw] * dtot_e[:, g * gw:(g + 1) * gw]
                      + _dot(bg.T.astype(BF16), xd_end[:, g * gw:(g + 1) * gw]))
        for hh in range(hpg):
            h = g * hpg + hh
            c = direction * SSD_HEADS + h
            diff = cs[:, c:c + 1] - cs_t[c:c + 1, :]
            decay = jnp.exp(jnp.where(tri, diff, NEG_BIG))
            m = (gram * decay).astype(BF16)
            y_d = _dot(m, xd[:, h * SSD_P:(h + 1) * SSD_P].astype(BF16))
            ys.append(y_d + y_off[:, hh * SSD_P:(hh + 1) * SSD_P])
    st_ref[...] = jnp.concatenate(new_st, axis=-1)
    y = jnp.concatenate(ys, axis=-1)
    if direction == 0:
        y_ref[...] = y
    else:
        z = z_ref[...]
        gated = (yf_ref[...] + y + xs * dsk_ref[...]) * _silu(z)
        y_ref[...] = _rms(gated, gn_ref[...]).astype(y_ref.dtype)


def _ssd(direction, xc, dt, dtb, a_neg, ex, n_batch, seq, ctx_len, extra=None):
    n = xc.shape[0]
    n_lat = n_batch * seq
    ncc = ctx_len // CHUNK
    nlc = seq // CHUNK

    def rows(b, j):
        if direction == 0:
            return jnp.where(j < ncc, n_lat // CHUNK + b * ncc + j, b * nlc + (j - ncc))
        return jnp.where(j < ncc, n_lat // CHUNK + b * ncc + (ncc - 1 - j), b * nlc + (nlc - 1 - (j - ncc)))

    const = lambda b, j: (0, 0)
    in_specs = [pl.BlockSpec((CHUNK, SSD_W), lambda b, j: (rows(b, j), 0)),
                pl.BlockSpec((CHUNK, SSD_G * SSD_N), lambda b, j: (rows(b, j), 2)),
                pl.BlockSpec((CHUNK, SSD_G * SSD_N), lambda b, j: (rows(b, j), 3)),
                pl.BlockSpec((CHUNK, LANES), lambda b, j: (rows(b, j), 0)),
                pl.BlockSpec((1, LANES), const),
                pl.BlockSpec((1, LANES), const),
                pl.BlockSpec((LANES, SSD_W), const)]
    args = [xc, xc, xc, dt, dtb, a_neg, ex]
    if direction == 1:
        yf, z, dskip, gnorm = extra
        in_specs += [pl.BlockSpec((CHUNK, SSD_W), lambda b, j: (rows(b, j), 0)),
                     pl.BlockSpec((CHUNK, SSD_W), lambda b, j: (rows(b, j), 0)),
                     pl.BlockSpec((1, SSD_W), const),
                     pl.BlockSpec((1, SSD_W), const)]
        args += [yf, z, dskip, gnorm]
    return pl.pallas_call(
        functools.partial(_ssd_kernel, direction),
        grid=(n_batch, ncc + nlc),
        in_specs=in_specs,
        out_specs=pl.BlockSpec((CHUNK, SSD_W), lambda b, j: (rows(b, j), 0)),
        out_shape=jax.ShapeDtypeStruct((n, SSD_W), F32 if direction == 0 else BF16),
        scratch_shapes=[pltpu.VMEM((SSD_N, SSD_W), F32)],
        compiler_params=_params("arbitrary", "arbitrary"),
        name="ssd_fwd" if direction == 0 else "ssd_bwd",
    )(*args)


def _route(logits, bias):
    tokens = logits.shape[1]
    scores = _sigmoid(logits)
    choice = scores + bias
    sub = lax.broadcasted_iota(jnp.int32, (GROUP_SIZE, tokens), 0).astype(F32)
    neg = -jnp.inf

    def pick_first_max(v, idx, sentinel):
        m = jnp.max(v, axis=0, keepdims=True)
        first = jnp.min(jnp.where(v == m, idx, sentinel), axis=0, keepdims=True)
        return m, idx == first

    group_scores = []
    for g in range(N_GROUPS):
        cg = choice[g * GROUP_SIZE:(g + 1) * GROUP_SIZE, :]
        m1, hit = pick_first_max(cg, sub, GROUP_SIZE)
        m2 = jnp.max(jnp.where(hit, neg, cg), axis=0, keepdims=True)
        group_scores.append(m1 + m2)
    v = jnp.concatenate(group_scores, axis=0)
    gsel = jnp.zeros_like(v)
    for _ in range(TOPK_GROUPS):
        _, hit = pick_first_max(v, sub, N_GROUPS)
        gsel = jnp.where(hit, 1.0, gsel)
        v = jnp.where(hit, neg, v)
    ok = jnp.concatenate([jnp.broadcast_to(gsel[g:g + 1, :], (GROUP_SIZE, tokens)) for g in range(N_GROUPS)],
                         axis=0) > 0.5
    v = jnp.where(ok, choice, neg)
    eidx = lax.broadcasted_iota(jnp.int32, (N_EXP, tokens), 0).astype(F32)
    ids = []
    gates = []
    for _ in range(TOP_K):
        _, hit = pick_first_max(v, eidx, N_EXP)
        ids.append(jnp.sum(jnp.where(hit, eidx, 0.0), axis=0, keepdims=True))
        gates.append(jnp.sum(jnp.where(hit, scores, 0.0), axis=0, keepdims=True))
        v = jnp.where(hit, neg, v)
    ids = jnp.concatenate(ids, axis=0)
    gates = jnp.concatenate(gates, axis=0)
    gates = gates / jnp.sum(gates, axis=0, keepdims=True) * ROUTED_SCALE
    return ids.astype(jnp.int32), gates


def _outproj_kernel(att_ref, ssd_ref, x_ref, wo_ref, mod_ref, gpost_ref, gpre_ref, wr_ref, rb_ref,
                    x1_ref, h2_ref, ids_ref, gates_ref):
    m = _dot(att_ref[...], wo_ref[0:ATT_W, :]) + _dot(ssd_ref[...], wo_ref[ATT_W:, :])
    x1 = x_ref[...] + mod_ref[0, 2:3, :] * _rms(m, gpost_ref[...])
    x1_ref[...] = x1
    h2 = _rms(x1, gpre_ref[...]) * (1.0 + mod_ref[0, 4:5, :]) + mod_ref[0, 3:4, :]
    h2_ref[...] = h2
    logits = lax.dot_general(wr_ref[...], h2, _NT, preferred_element_type=F32,
                             precision=lax.Precision.HIGHEST)
    ids_ref[...], gates_ref[...] = _route(logits, rb_ref[...])


def _outproj(att, ssd, xs, w_out, mod, g_post, g_pre, wr_t, r_bias, n_rows, n_batch, seq):
    d = xs.shape[1]
    n_lat = n_batch * seq
    tm = ROW_TILE
    row = lambda i: (i, 0)
    const = lambda i: (0, 0)
    return pl.pallas_call(
        _outproj_kernel,
        grid=(n_rows // tm,),
        in_specs=[pl.BlockSpec((tm, ATT_W), row),
                  pl.BlockSpec((tm, SSD_W), row),
                  pl.BlockSpec((tm, d), row),
                  pl.BlockSpec((ATT_W + SSD_W, d), const),
                  pl.BlockSpec((1, 6, d), lambda i: (_mod_row(i * tm, n_lat, seq, n_batch), 0, 0)),
                  pl.BlockSpec((1, d), const),
                  pl.BlockSpec((1, d), const),
                  pl.BlockSpec((N_EXP, d), const),
                  pl.BlockSpec((N_EXP, 1), const)],
        out_specs=[pl.BlockSpec((tm, d), row),
                   pl.BlockSpec((tm, d), row),
                   pl.BlockSpec((TOP_K, tm), lambda i: (0, i)),
                   pl.BlockSpec((TOP_K, tm), lambda i: (0, i))],
        out_shape=[jax.ShapeDtypeStruct((n_rows, d), F32),
                   jax.ShapeDtypeStruct((n_rows, d), F32),
                   jax.ShapeDtypeStruct((TOP_K, n_rows), jnp.int32),
                   jax.ShapeDtypeStruct((TOP_K, n_rows), F32)],
        compiler_params=_params("arbitrary"),
        name="outproj_router",
    )(att, ssd, xs, w_out, mod, g_post, g_pre, wr_t, r_bias)


def _moe_kernel(win, offs_ref, lst_ref, gts_ref, src_ref, wg_ref, wu_ref, wd_ref, o_ref, acc_ref, buf_ref, ybuf_ref):
    w = pl.program_id(0)
    e = pl.program_id(1)

    @pl.when(e == 0)
    def _():
        acc_ref[...] = jnp.zeros_like(acc_ref)
        buf_ref[...] = jnp.zeros_like(buf_ref)

    base = offs_ref[w * (N_EXP + 1) + e]
    cnt = offs_ref[w * (N_EXP + 1) + e + 1] - base
    trash = win

    def group(gi, carry):
        start = base + gi * GROUP
        n_valid = jnp.minimum(GROUP, cnt - gi * GROUP)
        n_blocks = (n_valid + UNROLL - 1) // UNROLL

        def gather(i, c):
            for u in range(UNROLL):
                r = i * UNROLL + u
                t = lst_ref[0, 0, start + r]
                buf_ref[pl.ds(pl.multiple_of(r * SUB, SUB), SUB), :] = src_ref[t]
            return c

        lax.fori_loop(0, n_blocks, gather, 0)
        x = jnp.concatenate([buf_ref[pl.ds(kc, GROUP, stride=SUB), :] for kc in range(SUB)], axis=1).astype(BF16)
        a = _silu(_dot(x, wg_ref[0])) * _dot(x, wu_ref[0])
        y = _dot(a.astype(BF16), wd_ref[0])
        for kc in range(SUB):
            ybuf_ref[pl.ds(kc, GROUP, stride=SUB), :] = y[:, kc * LANES:(kc + 1) * LANES]

        def scatter(i, c):
            updates = []
            for u in range(UNROLL):
                r = i * UNROLL + u
                t = jnp.where(r < n_valid, lst_ref[0, 0, start + r], trash)
                val = acc_ref[t] + ybuf_ref[pl.ds(pl.multiple_of(r * SUB, SUB), SUB), :] * gts_ref[0, 0, start + r]
                updates.append((t, val))
            for t, val in updates:
                acc_ref[t] = val
            return c

        lax.fori_loop(0, n_blocks, scatter, 0)
        return carry

    lax.fori_loop(0, (cnt + GROUP - 1) // GROUP, group, 0)

    @pl.when(e == N_EXP - 1)
    def _():
        o_ref[...] = acc_ref[0:win]


def _dispatch_tables(ids_t, gates_t, win):
    n = ids_t.shape[1]
    n_win = n // win
    pairs = n * TOP_K
    tok = jnp.arange(pairs, dtype=jnp.int32) // TOP_K
    key = (tok // win) * N_EXP + ids_t.T.reshape(pairs)
    order = jnp.argsort(key, stable=True)
    per_win = win * TOP_K
    lists = (tok[order] % win).reshape(n_win, per_win)
    gates = gates_t.T.reshape(pairs)[order].reshape(n_win, per_win)
    starts = jnp.searchsorted(key[order], jnp.arange(n_win * N_EXP + 1, dtype=jnp.int32)).astype(jnp.int32)
    w_ids = jnp.arange(n_win, dtype=jnp.int32)[:, None]
    offs = starts[w_ids * N_EXP + jnp.arange(N_EXP + 1, dtype=jnp.int32)[None, :]] - w_ids * per_win
    pad = ((0, 0), (0, GROUP))
    return offs.reshape(-1), jnp.pad(lists, pad), jnp.pad(gates, pad)


def _moe(h2, ids_t, gates_t, w_gate, w_up, w_down, win):
    n_rows, d = h2.shape
    n_win = n_rows // win
    fe = w_gate.shape[-1]
    offs, lists, gates = _dispatch_tables(ids_t, gates_t, win)
    width = lists.shape[1]
    slab = lambda w, e, offs: (w, 0, 0)
    smem = lambda w, e, offs: (w, 0, 0)
    routed = pl.pallas_call(
        functools.partial(_moe_kernel, win),
        grid_spec=pltpu.PrefetchScalarGridSpec(
            num_scalar_prefetch=1,
            grid=(n_win, N_EXP),
            in_specs=[pl.BlockSpec((1, 1, width), smem, memory_space=pltpu.SMEM),
                      pl.BlockSpec((1, 1, width), smem, memory_space=pltpu.SMEM),
                      pl.BlockSpec((win, SUB, LANES), slab),
                      pl.BlockSpec((1, d, fe), lambda w, e, offs: (e, 0, 0)),
                      pl.BlockSpec((1, d, fe), lambda w, e, offs: (e, 0, 0)),
                      pl.BlockSpec((1, fe, d), lambda w, e, offs: (e, 0, 0))],
            out_specs=pl.BlockSpec((win, SUB, LANES), slab),
            scratch_shapes=[pltpu.VMEM((win + SUB, SUB, LANES), F32),
                            pltpu.VMEM((GROUP * SUB, LANES), F32),
                            pltpu.VMEM((GROUP * SUB, LANES), F32)]),
        out_shape=jax.ShapeDtypeStruct((n_rows, SUB, LANES), F32),
        compiler_params=_params("arbitrary", "arbitrary"),
        name="moe_routed",
    )(offs, lists.reshape(n_win, 1, width), gates.reshape(n_win, 1, width), h2.reshape(n_rows, SUB, LANES),
      w_gate, w_up, w_down)
    return routed.reshape(n_rows, d)


def _ffn_out_kernel(r_ref, h_ref, x1_ref, mod_ref, gpost_ref, sg_ref, su_ref, sd_ref, o_ref):
    h = h_ref[...].astype(BF16)
    a = _silu(_dot(h, sg_ref[...])) * _dot(h, su_ref[...])
    y = r_ref[...] + _dot(a.astype(BF16), sd_ref[...])
    o_ref[...] = x1_ref[...] + mod_ref[0, 5:6, :] * _rms(y, gpost_ref[...])


def _ffn_out(routed, h2, x1, mod, g_post, ws_gate, ws_up, ws_down, n_batch, seq):
    n_rows, d = x1.shape
    n_lat = n_batch * seq
    tm = ROW_TILE
    fs = ws_gate.shape[-1]
    row = lambda i: (i, 0)
    const = lambda i: (0, 0)
    return pl.pallas_call(
        _ffn_out_kernel,
        grid=(n_rows // tm,),
        in_specs=[pl.BlockSpec((tm, d), row),
                  pl.BlockSpec((tm, d), row),
                  pl.BlockSpec((tm, d), row),
                  pl.BlockSpec((1, 6, d), lambda i: (_mod_row(i * tm, n_lat, seq, n_batch), 0, 0)),
                  pl.BlockSpec((1, d), const),
                  pl.BlockSpec((d, fs), const),
                  pl.BlockSpec((d, fs), const),
                  pl.BlockSpec((fs, d), const)],
        out_specs=pl.BlockSpec((tm, d), row),
        out_shape=jax.ShapeDtypeStruct((n_rows, d), F32),
        compiler_params=_params("arbitrary"),
        name="ffn_out",
    )(routed, h2, x1, mod, g_post, ws_gate, ws_up, ws_down)


def _rope_tables(seq, ctx_len):
    rows = seq // GRID_W
    row = jnp.repeat(jnp.arange(rows), GRID_W).astype(F32)
    col = jnp.tile(jnp.arange(GRID_W), rows).astype(F32)
    n_freq = HEAD_DIM // 4
    inv_freq = ROPE_THETA ** (-jnp.arange(n_freq, dtype=F32) / n_freq)
    ang = jnp.concatenate([row[:, None] * inv_freq, col[:, None] * inv_freq], axis=-1)
    reps = LANES // (HEAD_DIM // 2)
    cos = jnp.tile(jnp.cos(ang), (1, reps))
    sign = jnp.where((jnp.arange(LANES) % HEAD_DIM) < HEAD_DIM // 2, -1.0, 1.0).astype(F32)
    sin = jnp.tile(jnp.sin(ang), (1, reps)) * sign
    cos = jnp.concatenate([cos, jnp.ones((ctx_len, LANES), F32)], axis=0)
    sin = jnp.concatenate([sin, jnp.zeros((ctx_len, LANES), F32)], axis=0)
    return cos, sin


def _pad_lanes(v):
    return jnp.pad(v.reshape(1, -1), ((0, 0), (0, LANES - v.size)))


def kernel(x, c, ctx, c_ctx, w_mod, b_mod, g_pre_mix, g_post_mix, g_pre_ffn, g_post_ffn, w_in, q_norm, k_norm,
           conv_w, conv_b, dt_bias, a_log, d_skip, ssd_norm, w_out, router_w, router_bias, w_gate, w_up, w_down,
           ws_gate, ws_up, ws_down):
    n_batch, seq, d = x.shape
    ctx_len = ctx.shape[1]
    depth = w_mod.shape[0]
    n_lat = n_batch * seq
    assert n_batch < MOD_ROWS and seq % ROW_TILE == 0 and ctx_len % ROW_TILE == 0 and seq % GRID_W == 0

    xs = jnp.concatenate([x.reshape(n_lat, d), ctx.reshape(n_batch * ctx_len, d)], axis=0)
    cvec = jnp.zeros((MOD_ROWS, d), F32).at[:n_batch].set(c).at[n_batch].set(c_ctx)
    mod_all = _modulation(cvec, w_mod, b_mod)
    cos_t, sin_t = _rope_tables(seq, ctx_len)
    head_lane = jnp.arange(LANES)[:, None]
    chan_head = jnp.arange(SSD_W)[None, :] // SSD_P
    win = math.gcd(MAX_WINDOW, math.gcd(n_lat, n_batch * ctx_len))

    for i in range(depth):
        last = i == depth - 1
        mod = mod_all[i].reshape(MOD_ROWS, 6, d)
        w_in_b = jnp.pad(w_in[i].astype(BF16), ((0, 0), (0, IN_PAD - IN_W)))
        q, k, v, z, xbc, dt = _inproj(xs, mod, g_pre_mix[i].reshape(1, d), w_in_b,
                                      jnp.tile(q_norm[i], LANES // HEAD_DIM).reshape(1, LANES),
                                      jnp.tile(k_norm[i], LANES // HEAD_DIM).reshape(1, LANES),
                                      cos_t, sin_t, n_batch, seq, ctx_len)
        att = _attention(q, k, v, n_batch, seq, ctx_len, with_ctx=not last)
        xc = _conv(xbc, conv_w[i], conv_b[i].reshape(1, CONV_W), n_batch, seq, ctx_len)
        dtb = _pad_lanes(dt_bias[i])
        a_neg = _pad_lanes(-jnp.exp(a_log[i]))
        y_f = _ssd(0, xc, dt, dtb, a_neg, (head_lane == chan_head).astype(BF16), n_batch, seq, ctx_len)
        extra = (y_f, z, jnp.repeat(d_skip[i], SSD_P).reshape(1, SSD_W), ssd_norm[i].reshape(1, SSD_W))
        ssd = _ssd(1, xc, dt, dtb, a_neg, (head_lane == chan_head + SSD_HEADS).astype(BF16),
                   n_batch, seq, ctx_len, extra)
        n_rows = n_lat if last else xs.shape[0]
        x1, h2, ids_t, gates_t = _outproj(att, ssd, xs, w_out[i].astype(BF16), mod, g_post_mix[i].reshape(1, d),
                                          g_pre_ffn[i].reshape(1, d), router_w[i].T,
                                          router_bias[i].reshape(N_EXP, 1), n_rows, n_batch, seq)
        routed = _moe(h2, ids_t, gates_t, w_gate[i].astype(BF16), w_up[i].astype(BF16), w_down[i].astype(BF16), win)
        xs = _ffn_out(routed, h2, x1, mod, g_post_ffn[i].reshape(1, d), ws_gate[i].astype(BF16),
                      ws_up[i].astype(BF16), ws_down[i].astype(BF16), n_batch, seq)
    return xs.reshape(n_batch, seq, d)
```

```python
import functools
import math

import jax
import jax.numpy as jnp
from jax import lax
from jax.experimental import pallas as pl
from jax.experimental.pallas import tpu as pltpu

F32 = jnp.float32
BF16 = jnp.bfloat16

GRID_W = 64
HEADS = 8
KV_HEADS = 2
HEAD_DIM = 64
Q_PER_KV = HEADS // KV_HEADS
ATT_W = HEADS * HEAD_DIM
ROPE_THETA = 10000.0
SSD_HEADS = 8
SSD_P = 64
SSD_W = SSD_HEADS * SSD_P
SSD_G = 2
SSD_N = 128
CHUNK = 128
CONV_W = SSD_W + 2 * SSD_G * SSD_N
IN_W = ATT_W + 2 * KV_HEADS * HEAD_DIM + SSD_W + CONV_W + 2 * SSD_HEADS
LANES = 128
IN_PAD = IN_W - 2 * SSD_HEADS + LANES
N_EXP = 64
TOP_K = 8
N_GROUPS = 8
TOPK_GROUPS = 4
GROUP_SIZE = N_EXP // N_GROUPS
ROUTED_SCALE = 2.5
EPS = 1e-6
MOD_ROWS = 16
ROW_TILE = 256
MAX_WINDOW = 2048
GROUP = 128
SCATTER_BATCH = 16
RANK_CHUNK = 256
TOK_SPLIT = 64
FFN_TILE = 512
SUB = 8
VMEM_LIMIT = 56 * 1024 * 1024
NEG_BIG = -1e30

_NT = (((1,), (1,)), ((), ()))


def _params(*sem):
    return pltpu.CompilerParams(dimension_semantics=sem, vmem_limit_bytes=VMEM_LIMIT)


def _sigmoid(v):
    return 1.0 / (1.0 + jnp.exp(-v))


def _silu(v):
    return v * _sigmoid(v)


def _rms(v, gain):
    return v * lax.rsqrt(jnp.mean(v * v, axis=-1, keepdims=True) + EPS) * gain


def _dot(a, b):
    return jnp.dot(a, b, preferred_element_type=F32)


def _split2(v):
    hi = v.astype(BF16)
    lo = (v - hi.astype(F32)).astype(BF16)
    return hi, lo


def _split3(v):
    hi = v.astype(BF16)
    r = v - hi.astype(F32)
    mid = r.astype(BF16)
    lo = (r - mid.astype(F32)).astype(BF16)
    return hi, mid, lo


def _mod_kernel(c_ref, w_ref, b_ref, o_ref):
    s = _silu(c_ref[...])
    o_ref[0] = jnp.dot(s, w_ref[0], preferred_element_type=F32, precision=lax.Precision.HIGHEST) + b_ref[0]


def _modulation(cvec, w_mod, b_mod):
    depth, d, six_d = w_mod.shape
    tn = six_d // 4
    return pl.pallas_call(
        _mod_kernel,
        grid=(depth, six_d // tn),
        in_specs=[pl.BlockSpec((MOD_ROWS, d), lambda l, j: (0, 0)),
                  pl.BlockSpec((1, d, tn), lambda l, j: (l, 0, j)),
                  pl.BlockSpec((1, 1, tn), lambda l, j: (l, 0, j))],
        out_specs=pl.BlockSpec((1, MOD_ROWS, tn), lambda l, j: (l, 0, j)),
        out_shape=jax.ShapeDtypeStruct((depth, MOD_ROWS, six_d), F32),
        compiler_params=_params("arbitrary", "arbitrary"),
        name="modulation",
    )(cvec, w_mod, b_mod.reshape(depth, 1, six_d))


def _head_norm_rope(xb, gain, cos, sin):
    lane = lax.broadcasted_iota(jnp.int32, xb.shape, 1)
    low = lane < HEAD_DIM
    sq = xb * xb
    s_lo = jnp.sum(jnp.where(low, sq, 0.0), axis=-1, keepdims=True)
    s_hi = jnp.sum(jnp.where(low, 0.0, sq), axis=-1, keepdims=True)
    ms = jnp.where(low, s_lo, s_hi) * (1.0 / HEAD_DIM)
    y = xb * lax.rsqrt(ms + EPS) * gain
    ahead = pltpu.roll(y, LANES - HEAD_DIM // 2, 1)
    behind = pltpu.roll(y, HEAD_DIM // 2, 1)
    first_half = (lane % HEAD_DIM) < (HEAD_DIM // 2)
    return y * cos + jnp.where(first_half, ahead, behind) * sin


def _inproj_kernel(x_ref, mod_ref, g_ref, w_ref, qg_ref, kg_ref, cos_ref, sin_ref,
                   q_ref, k_ref, v_ref, z_ref, xbc_ref, dt_ref):
    h = _rms(x_ref[...], g_ref[...]) * (1.0 + mod_ref[0, 1:2, :]) + mod_ref[0, 0:1, :]
    hb = h.astype(BF16)
    cos = cos_ref[...]
    sin = sin_ref[...]
    c0 = 0
    for blk in range(ATT_W // LANES):
        qb = _dot(hb, w_ref[:, c0:c0 + LANES])
        qb = _head_norm_rope(qb, qg_ref[...], cos, sin) * (HEAD_DIM ** -0.5)
        q_ref[:, c0:c0 + LANES] = qb.astype(q_ref.dtype)
        c0 += LANES
    kb = _head_norm_rope(_dot(hb, w_ref[:, c0:c0 + LANES]), kg_ref[...], cos, sin)
    c0 += LANES
    vb = _dot(hb, w_ref[:, c0:c0 + LANES])
    c0 += LANES
    for g in range(KV_HEADS):
        k_ref[g] = kb[:, g * HEAD_DIM:(g + 1) * HEAD_DIM].astype(k_ref.dtype)
        v_ref[g] = vb[:, g * HEAD_DIM:(g + 1) * HEAD_DIM].astype(v_ref.dtype)
    z_ref[...] = _dot(hb, w_ref[:, c0:c0 + SSD_W])
    c0 += SSD_W
    xbc_ref[...] = _dot(hb, w_ref[:, c0:c0 + CONV_W])
    c0 += CONV_W
    dt_ref[...] = _dot(hb, w_ref[:, c0:c0 + LANES])


def _mod_row(start_row, n_lat, seq, n_batch):
    return jnp.where(start_row < n_lat, start_row // seq, n_batch)


def _inproj(xs, mod, g_pre, w_in, q_gain, k_gain, cos_t, sin_t, n_batch, seq, ctx_len):
    n, d = xs.shape
    n_lat = n_batch * seq
    tm = ROW_TILE
    lat_tiles = seq // tm
    ctx_tiles = ctx_len // tm

    def tab_idx(i):
        return (jnp.where(i * tm < n_lat, i % lat_tiles, lat_tiles + (i - n_lat // tm) % ctx_tiles), 0)

    row = lambda i: (i, 0)
    const = lambda i: (0, 0)
    return pl.pallas_call(
        _inproj_kernel,
        grid=(n // tm,),
        in_specs=[pl.BlockSpec((tm, d), row),
                  pl.BlockSpec((1, 6, d), lambda i: (_mod_row(i * tm, n_lat, seq, n_batch), 0, 0)),
                  pl.BlockSpec((1, d), const),
                  pl.BlockSpec((d, IN_PAD), const),
                  pl.BlockSpec((1, LANES), const),
                  pl.BlockSpec((1, LANES), const),
                  pl.BlockSpec((tm, LANES), tab_idx),
                  pl.BlockSpec((tm, LANES), tab_idx)],
        out_specs=[pl.BlockSpec((tm, ATT_W), row),
                   pl.BlockSpec((KV_HEADS, tm, HEAD_DIM), lambda i: (0, i, 0)),
                   pl.BlockSpec((KV_HEADS, tm, HEAD_DIM), lambda i: (0, i, 0)),
                   pl.BlockSpec((tm, SSD_W), row),
                   pl.BlockSpec((tm, CONV_W), row),
                   pl.BlockSpec((tm, LANES), row)],
        out_shape=[jax.ShapeDtypeStruct((n, ATT_W), BF16),
                   jax.ShapeDtypeStruct((KV_HEADS, n, HEAD_DIM), BF16),
                   jax.ShapeDtypeStruct((KV_HEADS, n, HEAD_DIM), BF16),
                   jax.ShapeDtypeStruct((n, SSD_W), F32),
                   jax.ShapeDtypeStruct((n, CONV_W), F32),
                   jax.ShapeDtypeStruct((n, LANES), F32)],
        compiler_params=_params("arbitrary"),
        name="inproj",
    )(xs, mod, g_pre, w_in, q_gain, k_gain, cos_t, sin_t)


def _attn_heads(q_ref, kc_ref, vc_ref, kl_ref, vl_ref, o_ref, latent):
    outs = []
    q = q_ref[...]
    for h in range(HEADS):
        g = h // Q_PER_KV
        qh = q[:, h * HEAD_DIM:(h + 1) * HEAD_DIM]
        sc = lax.dot_general(qh, kc_ref[g], _NT, preferred_element_type=F32)
        m = jnp.max(sc, axis=-1, keepdims=True)
        if latent:
            sl = lax.dot_general(qh, kl_ref[g], _NT, preferred_element_type=F32)
            m = jnp.maximum(m, jnp.max(sl, axis=-1, keepdims=True))
            pw = jnp.exp(sl - m)
        pc = jnp.exp(sc - m)
        den = jnp.sum(pc, axis=-1, keepdims=True)
        acc = _dot(pc.astype(BF16), vc_ref[g])
        if latent:
            den = den + jnp.sum(pw, axis=-1, keepdims=True)
            acc = acc + _dot(pw.astype(BF16), vl_ref[g])
        outs.append(acc / den)
    o_ref[...] = jnp.concatenate(outs, axis=-1).astype(o_ref.dtype)


def _attn_kernel(n_lat_q, with_ctx, q_ref, kc_ref, vc_ref, kl_ref, vl_ref, o_ref):
    if not with_ctx:
        _attn_heads(q_ref, kc_ref, vc_ref, kl_ref, vl_ref, o_ref, True)
        return
    j = pl.program_id(1)

    @pl.when(j < n_lat_q)
    def _():
        _attn_heads(q_ref, kc_ref, vc_ref, kl_ref, vl_ref, o_ref, True)

    @pl.when(j >= n_lat_q)
    def _():
        _attn_heads(q_ref, kc_ref, vc_ref, kl_ref, vl_ref, o_ref, False)


def _attention(q, k, v, n_batch, seq, ctx_len, with_ctx):
    n = q.shape[0]
    n_lat = n_batch * seq
    tq = ROW_TILE
    n_lat_q = seq // tq
    n_ctx_q = ctx_len // tq if with_ctx else 0

    def q_idx(b, j):
        return (jnp.where(j < n_lat_q, b * n_lat_q + j, n_lat // tq + b * (ctx_len // tq) + (j - n_lat_q)), 0)

    ctx_idx = lambda b, j: (0, n_lat // ctx_len + b, 0)
    lat_idx = lambda b, j: (0, b, 0)
    return pl.pallas_call(
        functools.partial(_attn_kernel, n_lat_q, with_ctx),
        grid=(n_batch, n_lat_q + n_ctx_q),
        in_specs=[pl.BlockSpec((tq, ATT_W), q_idx),
                  pl.BlockSpec((KV_HEADS, ctx_len, HEAD_DIM), ctx_idx),
                  pl.BlockSpec((KV_HEADS, ctx_len, HEAD_DIM), ctx_idx),
                  pl.BlockSpec((KV_HEADS, seq, HEAD_DIM), lat_idx),
                  pl.BlockSpec((KV_HEADS, seq, HEAD_DIM), lat_idx)],
        out_specs=pl.BlockSpec((tq, ATT_W), q_idx),
        out_shape=jax.ShapeDtypeStruct((n, ATT_W), BF16),
        compiler_params=_params("arbitrary", "arbitrary"),
        name="attention",
    )(q, k, v, k, v)


def _conv_kernel(lat_tiles, ctx_tiles, n_lat_tiles, x_ref, p_ref, nx_ref, w_ref, b_ref, o_ref):
    i = pl.program_id(0)
    is_lat = i < n_lat_tiles
    pos = jnp.where(is_lat, i % lat_tiles, (i - n_lat_tiles) % ctx_tiles)
    last_pos = jnp.where(is_lat, lat_tiles - 1, ctx_tiles - 1)
    u = x_ref[...]
    rows = u.shape[0]
    r = lax.broadcasted_iota(jnp.int32, u.shape, 0)
    prev_row = jnp.where(pos == 0, 0.0, p_ref[7:8, :])
    next_row = jnp.where(pos == last_pos, 0.0, nx_ref[0:1, :])
    up = jnp.where(r == 0, prev_row, pltpu.roll(u, 1, 0))
    dn = jnp.where(r == rows - 1, next_row, pltpu.roll(u, rows - 1, 0))
    y = w_ref[0:1, :] * up + w_ref[1:2, :] * u + w_ref[2:3, :] * dn + b_ref[...]
    o_ref[...] = _silu(y)


def _conv(xbc, conv_w, conv_b, n_batch, seq, ctx_len):
    n, cw = xbc.shape
    tm = ROW_TILE
    sub = 8
    per = tm // sub
    return pl.pallas_call(
        functools.partial(_conv_kernel, seq // tm, ctx_len // tm, n_batch * seq // tm),
        grid=(n // tm,),
        in_specs=[pl.BlockSpec((tm, cw), lambda i: (i, 0)),
                  pl.BlockSpec((sub, cw), lambda i: (jnp.maximum(i * per - 1, 0), 0)),
                  pl.BlockSpec((sub, cw), lambda i: (jnp.minimum((i + 1) * per, n // sub - 1), 0)),
                  pl.BlockSpec((3, cw), lambda i: (0, 0)),
                  pl.BlockSpec((1, cw), lambda i: (0, 0))],
        out_specs=pl.BlockSpec((tm, cw), lambda i: (i, 0)),
        out_shape=jax.ShapeDtypeStruct((n, cw), F32),
        compiler_params=_params("arbitrary"),
        name="conv",
    )(xbc, xbc, xbc, conv_w, conv_b)


def _ssd_kernel(direction, *refs):
    if direction == 0:
        xs_ref, b_ref, c_ref, dt_ref, dtb_ref, a_ref, ex_ref, y_ref, st_ref = refs
    else:
        (xs_ref, b_ref, c_ref, dt_ref, dtb_ref, a_ref, ex_ref, yf_ref, z_ref, dsk_ref, gn_ref,
         y_ref, st_ref) = refs

    @pl.when(pl.program_id(1) == 0)
    def _():
        st_ref[...] = jnp.zeros_like(st_ref)

    xs = xs_ref[...]
    pre = dt_ref[...] + dtb_ref[...]
    dtv = jnp.maximum(pre, 0.0) + jnp.log1p(jnp.exp(-jnp.abs(pre)))
    la = dtv * a_ref[...]
    row = lax.broadcasted_iota(jnp.int32, (CHUNK, CHUNK), 0)
    col = lax.broadcasted_iota(jnp.int32, (CHUNK, CHUNK), 1)
    tri = (col <= row) if direction == 0 else (col >= row)
    tri_b = jnp.where(tri, 1.0, 0.0).astype(BF16)
    cs = sum(_dot(tri_b, part) for part in _split3(la))
    ex = ex_ref[...]

    def expand(val):
        return sum(_dot(part, ex) for part in _split2(val))

    edge = CHUNK - 1 if direction == 0 else 0
    tot = cs[edge:edge + 1, :]
    dt_e = expand(dtv)
    da_e = expand(jnp.exp(cs))
    db_e = expand(jnp.exp(tot - cs))
    dtot_e = da_e[edge:edge + 1, :]
    xd = xs * dt_e
    xd_end = (xd * db_e).astype(BF16)
    cs_t = cs.T
    st = st_ref[...]
    st_b = st.astype(BF16)
    ys = []
    new_st = []
    gw = SSD_W // SSD_G
    hpg = SSD_HEADS // SSD_G
    for g in range(SSD_G):
        bg = b_ref[:, g * SSD_N:(g + 1) * SSD_N]
        cg = c_ref[:, g * SSD_N:(g + 1) * SSD_N].astype(BF16)
        gram = lax.dot_general(cg, bg.astype(BF16), _NT, preferred_element_type=F32)
        y_off = _dot(cg, st_b[:, g * gw:(g + 1) * gw]) * da_e[:, g * gw:(g + 1) * gw]
        new_st.append(st[:, g * gw:(g + 1) * gw] * dtot_e[:, g * gw:(g + 1) * gw]
                      + _dot(bg.T.astype(BF16), xd_end[:, g * gw:(g + 1) * gw]))
        for hh in range(hpg):
            h = g * hpg + hh
            c = direction * SSD_HEADS + h
            diff = cs[:, c:c + 1] - cs_t[c:c + 1, :]
            decay = jnp.exp(jnp.where(tri, diff, NEG_BIG))
            m = (gram * decay).astype(BF16)
            y_d = _dot(m, xd[:, h * SSD_P:(h + 1) * SSD_P].astype(BF16))
            ys.append(y_d + y_off[:, hh * SSD_P:(hh + 1) * SSD_P])
    st_ref[...] = jnp.concatenate(new_st, axis=-1)
    y = jnp.concatenate(ys, axis=-1)
    if direction == 0:
        y_ref[...] = y
    else:
        z = z_ref[...]
        gated = (yf_ref[...] + y + xs * dsk_ref[...]) * _silu(z)
        y_ref[...] = _rms(gated, gn_ref[...]).astype(y_ref.dtype)


def _ssd(direction, xc, dt, dtb, a_neg, ex, n_batch, seq, ctx_len, extra=None):
    n = xc.shape[0]
    n_lat = n_batch * seq
    ncc = ctx_len // CHUNK
    nlc = seq // CHUNK

    def rows(b, j):
        if direction == 0:
            return jnp.where(j < ncc, n_lat // CHUNK + b * ncc + j, b * nlc + (j - ncc))
        return jnp.where(j < ncc, n_lat // CHUNK + b * ncc + (ncc - 1 - j), b * nlc + (nlc - 1 - (j - ncc)))

    const = lambda b, j: (0, 0)
    in_specs = [pl.BlockSpec((CHUNK, SSD_W), lambda b, j: (rows(b, j), 0)),
                pl.BlockSpec((CHUNK, SSD_G * SSD_N), lambda b, j: (rows(b, j), 2)),
                pl.BlockSpec((CHUNK, SSD_G * SSD_N), lambda b, j: (rows(b, j), 3)),
                pl.BlockSpec((CHUNK, LANES), lambda b, j: (rows(b, j), 0)),
                pl.BlockSpec((1, LANES), const),
                pl.BlockSpec((1, LANES), const),
                pl.BlockSpec((LANES, SSD_W), const)]
    args = [xc, xc, xc, dt, dtb, a_neg, ex]
    if direction == 1:
        yf, z, dskip, gnorm = extra
        in_specs += [pl.BlockSpec((CHUNK, SSD_W), lambda b, j: (rows(b, j), 0)),
                     pl.BlockSpec((CHUNK, SSD_W), lambda b, j: (rows(b, j), 0)),
                     pl.BlockSpec((1, SSD_W), const),
                     pl.BlockSpec((1, SSD_W), const)]
        args += [yf, z, dskip, gnorm]
    return pl.pallas_call(
        functools.partial(_ssd_kernel, direction),
        grid=(n_batch, ncc + nlc),
        in_specs=in_specs,
        out_specs=pl.BlockSpec((CHUNK, SSD_W), lambda b, j: (rows(b, j), 0)),
        out_shape=jax.ShapeDtypeStruct((n, SSD_W), F32 if direction == 0 else BF16),
        scratch_shapes=[pltpu.VMEM((SSD_N, SSD_W), F32)],
        compiler_params=_params("arbitrary", "arbitrary"),
        name="ssd_fwd" if direction == 0 else "ssd_bwd",
    )(*args)


def _route(logits, bias):
    tokens = logits.shape[1]
    scores = _sigmoid(logits)
    choice = scores + bias
    sub = lax.broadcasted_iota(jnp.int32, (GROUP_SIZE, tokens), 0).astype(F32)
    neg = -jnp.inf

    def pick_first_max(v, idx, sentinel):
        m = jnp.max(v, axis=0, keepdims=True)
        first = jnp.min(jnp.where(v == m, idx, sentinel), axis=0, keepdims=True)
        return m, idx == first

    group_scores = []
    for g in range(N_GROUPS):
        cg = choice[g * GROUP_SIZE:(g + 1) * GROUP_SIZE, :]
        m1, hit = pick_first_max(cg, sub, GROUP_SIZE)
        m2 = jnp.max(jnp.where(hit, neg, cg), axis=0, keepdims=True)
        group_scores.append(m1 + m2)
    v = jnp.concatenate(group_scores, axis=0)
    gsel = jnp.zeros_like(v)
    for _ in range(TOPK_GROUPS):
        _, hit = pick_first_max(v, sub, N_GROUPS)
        gsel = jnp.where(hit, 1.0, gsel)
        v = jnp.where(hit, neg, v)
    ok = jnp.concatenate([jnp.broadcast_to(gsel[g:g + 1, :], (GROUP_SIZE, tokens)) for g in range(N_GROUPS)],
                         axis=0) > 0.5
    v = jnp.where(ok, choice, neg)
    eidx = lax.broadcasted_iota(jnp.int32, (N_EXP, tokens), 0).astype(F32)
    ids = []
    gates = []
    for _ in range(TOP_K):
        _, hit = pick_first_max(v, eidx, N_EXP)
        ids.append(jnp.sum(jnp.where(hit, eidx, 0.0), axis=0, keepdims=True))
        gates.append(jnp.sum(jnp.where(hit, scores, 0.0), axis=0, keepdims=True))
        v = jnp.where(hit, neg, v)
    ids = jnp.concatenate(ids, axis=0)
    gates = jnp.concatenate(gates, axis=0)
    gates = gates / jnp.sum(gates, axis=0, keepdims=True) * ROUTED_SCALE
    return ids.astype(jnp.int32), gates


def _store_slabs(slab_ref, rows):
    for kc in range(SUB):
        slab_ref[pl.ds(kc, rows.shape[0], stride=SUB), :] = rows[:, kc * LANES:(kc + 1) * LANES]


def _load_slabs(slab_ref, n):
    return jnp.concatenate([slab_ref[pl.ds(kc, n, stride=SUB), :] for kc in range(SUB)], axis=1)


def _outproj_kernel(att_ref, ssd_ref, x_ref, wo_ref, mod_ref, gpost_ref, gpre_ref, wrh_ref, wrl_ref, rb_ref,
                    x1_ref, h2_ref, ids_ref, gates_ref):
    m = _dot(att_ref[...], wo_ref[0:ATT_W, :]) + _dot(ssd_ref[...], wo_ref[ATT_W:, :])
    x1 = x_ref[...] + mod_ref[0, 2:3, :] * _rms(m, gpost_ref[...])
    x1_ref[...] = x1
    h2 = _rms(x1, gpre_ref[...]) * (1.0 + mod_ref[0, 4:5, :]) + mod_ref[0, 3:4, :]
    _store_slabs(h2_ref, h2)
    h_hi, h_lo = _split2(h2)
    nt = lambda a, b: lax.dot_general(a, b, _NT, preferred_element_type=F32)
    logits = nt(wrh_ref[...], h_hi) + (nt(wrh_ref[...], h_lo) + nt(wrl_ref[...], h_hi))
    ids_ref[...], gates_ref[...] = _route(logits, rb_ref[...])


def _outproj(att, ssd, xs, w_out, mod, g_post, g_pre, wr_t, r_bias, n_rows, n_batch, seq):
    d = xs.shape[1]
    n_lat = n_batch * seq
    tm = FFN_TILE
    wr_hi = wr_t.astype(BF16)
    wr_lo = (wr_t - wr_hi.astype(F32)).astype(BF16)
    row = lambda i: (i, 0)
    const = lambda i: (0, 0)
    return pl.pallas_call(
        _outproj_kernel,
        grid=(n_rows // tm,),
        in_specs=[pl.BlockSpec((tm, ATT_W), row),
                  pl.BlockSpec((tm, SSD_W), row),
                  pl.BlockSpec((tm, d), row),
                  pl.BlockSpec((ATT_W + SSD_W, d), const),
                  pl.BlockSpec((1, 6, d), lambda i: (_mod_row(i * tm, n_lat, seq, n_batch), 0, 0)),
                  pl.BlockSpec((1, d), const),
                  pl.BlockSpec((1, d), const),
                  pl.BlockSpec((N_EXP, d), const),
                  pl.BlockSpec((N_EXP, d), const),
                  pl.BlockSpec((N_EXP, 1), const)],
        out_specs=[pl.BlockSpec((tm, d), row),
                   pl.BlockSpec((tm * SUB, LANES), row),
                   pl.BlockSpec((TOP_K, tm), lambda i: (0, i)),
                   pl.BlockSpec((TOP_K, tm), lambda i: (0, i))],
        out_shape=[jax.ShapeDtypeStruct((n_rows, d), F32),
                   jax.ShapeDtypeStruct((n_rows * SUB, LANES), F32),
                   jax.ShapeDtypeStruct((TOP_K, n_rows), jnp.int32),
                   jax.ShapeDtypeStruct((TOP_K, n_rows), F32)],
        compiler_params=_params("arbitrary"),
        name="outproj_router",
    )(att, ssd, xs, w_out, mod, g_post, g_pre, wr_hi, wr_lo, r_bias)


def _tables_kernel(win, n_groups, ids_ref, gates_ref, list_ref, gl_ref, first_ref, ngr_ref):
    ids = ids_ref[...]
    gts = gates_ref[...]
    eidx = lax.broadcasted_iota(jnp.int32, (N_EXP, win), 0)
    hits = [ids[k:k + 1, :] == eidx for k in range(TOP_K)]
    sel = sum(jnp.where(h, 1.0, 0.0) for h in hits)
    cnt = jnp.sum(sel, axis=1, keepdims=True)
    ngr = jnp.floor((cnt + (GROUP - 1)) * (1.0 / GROUP))
    er = lax.broadcasted_iota(jnp.int32, (N_EXP, N_EXP), 0)
    ec = lax.broadcasted_iota(jnp.int32, (N_EXP, N_EXP), 1)
    below = jnp.where(ec < er, 1.0, 0.0).astype(BF16)
    ngr_b = jnp.broadcast_to(ngr, (N_EXP, LANES))
    first = _dot(below, ngr_b.astype(BF16)) + 1.0
    cr = lax.broadcasted_iota(jnp.int32, (RANK_CHUNK, RANK_CHUNK), 0)
    cc = lax.broadcasted_iota(jnp.int32, (RANK_CHUNK, RANK_CHUNK), 1)
    before = jnp.where(cr < cc, 1.0, 0.0).astype(BF16)
    carry = jnp.zeros((N_EXP, 1), F32)
    ranks = []
    for c0 in range(0, win, RANK_CHUNK):
        s = sel[:, c0:c0 + RANK_CHUNK]
        ranks.append(_dot(s.astype(BF16), before) + carry)
        carry = carry + jnp.sum(s, axis=1, keepdims=True)
    slot_of = first[:, 0:1] * GROUP + jnp.concatenate(ranks, axis=1)
    slot = jnp.concatenate([jnp.sum(jnp.where(h, slot_of, 0.0), axis=0, keepdims=True) for h in hits], axis=0)
    hi = jnp.floor(slot * (1.0 / GROUP))
    lo = slot - hi * GROUP
    g1 = gts.astype(BF16).astype(F32)
    rest = gts - g1
    g2 = rest.astype(BF16).astype(F32)
    g3 = rest - g2
    stack = jnp.concatenate([lo, g1, g2, g3, jnp.zeros((LANES - 4 * TOP_K, win), F32)], axis=0)
    cols = stack.T
    tok1 = lax.broadcasted_iota(jnp.int32, (win, LANES), 0) + 1
    tok_a = (tok1 // TOK_SPLIT).astype(F32)
    tok_b = (tok1 % TOK_SPLIT).astype(F32)
    lane = lax.broadcasted_iota(jnp.int32, (win, LANES), 1).astype(F32)
    gidx = lax.broadcasted_iota(jnp.int32, (n_groups, win), 0).astype(F32)
    acc = jnp.zeros((n_groups, 5 * LANES), F32)
    for k in range(TOP_K):
        onehot = cols[:, k:k + 1] == lane
        vals = (tok_a, tok_b, cols[:, TOP_K + k:TOP_K + k + 1], cols[:, 2 * TOP_K + k:2 * TOP_K + k + 1],
                cols[:, 3 * TOP_K + k:3 * TOP_K + k + 1])
        rhs = jnp.concatenate([jnp.where(onehot, v, 0.0) for v in vals], axis=1).astype(BF16)
        lhs = jnp.where(hi[k:k + 1, :] == gidx, 1.0, 0.0).astype(BF16)
        acc = acc + _dot(lhs, rhs)
    tok1_tab = acc[:, 0:LANES] * TOK_SPLIT + acc[:, LANES:2 * LANES]
    gate_tab = acc[:, 2 * LANES:3 * LANES] + acc[:, 3 * LANES:4 * LANES] + acc[:, 4 * LANES:5 * LANES]
    spare = lax.broadcasted_iota(jnp.int32, (n_groups, LANES), 1).astype(F32) + win
    row_of = jnp.where(tok1_tab > 0.5, tok1_tab - 1.0, spare) * SUB
    list_ref[0] = row_of.astype(jnp.int32)
    gl_ref[0] = gate_tab
    first_ref[0] = first.astype(jnp.int32)
    ngr_ref[0] = ngr_b.astype(jnp.int32)


def _tables(ids_t, gates_t, win):
    n = ids_t.shape[1]
    n_win = n // win
    n_groups = win * TOP_K // GROUP + N_EXP + SUB
    blk = lambda w: (0, w)
    out = lambda w: (w, 0, 0)
    return pl.pallas_call(
        functools.partial(_tables_kernel, win, n_groups),
        grid=(n_win,),
        in_specs=[pl.BlockSpec((TOP_K, win), blk), pl.BlockSpec((TOP_K, win), blk)],
        out_specs=[pl.BlockSpec((1, n_groups, LANES), out), pl.BlockSpec((1, n_groups, LANES), out),
                   pl.BlockSpec((1, N_EXP, LANES), out), pl.BlockSpec((1, N_EXP, LANES), out)],
        out_shape=[jax.ShapeDtypeStruct((n_win, n_groups, LANES), jnp.int32),
                   jax.ShapeDtypeStruct((n_win, n_groups, LANES), F32),
                   jax.ShapeDtypeStruct((n_win, N_EXP, LANES), jnp.int32),
                   jax.ShapeDtypeStruct((n_win, N_EXP, LANES), jnp.int32)],
        compiler_params=_params("arbitrary"),
        name="dispatch_tables",
    )(ids_t, gates_t)


def _moe_kernel(win, first_ref, ngr_ref, lst_ref, gl_ref, src_ref, wg_ref, wu_ref, wd_ref, o_ref,
                acc_ref, buf0_ref, buf1_ref, ybuf0_ref, ybuf1_ref):
    w = pl.program_id(0)
    e = pl.program_id(1)
    last_row = (win - 1) * SUB
    bufs = (buf0_ref, buf1_ref)
    ybufs = (ybuf0_ref, ybuf1_ref)

    def gather(g, buf_ref):
        for r in range(GROUP):
            row = pl.multiple_of(jnp.minimum(lst_ref[0, g, r], last_row), SUB)
            buf_ref[pl.ds(r * SUB, SUB), :] = src_ref[pl.ds(row, SUB), :]

    def scatter(g, ybuf_ref):
        for r0 in range(0, GROUP, SCATTER_BATCH):
            rows = [pl.multiple_of(lst_ref[0, g, r0 + u], SUB) for u in range(SCATTER_BATCH)]
            vals = [acc_ref[pl.ds(rows[u], SUB), :] + ybuf_ref[pl.ds((r0 + u) * SUB, SUB), :]
                    for u in range(SCATTER_BATCH)]
            for u in range(SCATTER_BATCH):
                acc_ref[pl.ds(rows[u], SUB), :] = vals[u]

    def expert(g, buf_ref, ybuf_ref):
        x = _load_slabs(buf_ref, GROUP).astype(BF16)
        a = _silu(_dot(x, wg_ref[0])) * _dot(x, wu_ref[0])
        gate_row = jnp.broadcast_to(gl_ref[0, pl.ds(g, 1), :], (GROUP, LANES))
        diag = (lax.broadcasted_iota(jnp.int32, (GROUP, LANES), 0)
                == lax.broadcasted_iota(jnp.int32, (GROUP, LANES), 1))
        gate_col = jnp.sum(jnp.where(diag, gate_row, 0.0), axis=1, keepdims=True)
        _store_slabs(ybuf_ref, _dot((a * gate_col).astype(BF16), wd_ref[0]))

    def stage(g, p):
        gather(g + 1, bufs[1 - p])
        expert(g, bufs[p], ybufs[p])
        scatter(g - 1, ybufs[1 - p])

    g0 = first_ref[w * N_EXP + e]
    n = ngr_ref[w * N_EXP + e]

    @pl.when(e == 0)
    def _():
        acc_ref[...] = jnp.zeros_like(acc_ref)
        ybuf0_ref[...] = jnp.zeros_like(ybuf0_ref)
        gather(1, bufs[1])

    def body(i, carry):
        g = g0 + i
        for p in range(2):
            pl.when(g % 2 == p)(functools.partial(stage, g, p))
        return carry

    lax.fori_loop(0, n, body, 0)

    @pl.when(e == N_EXP - 1)
    def _():
        g_last = g0 + n - 1
        for p in range(2):
            pl.when(g_last % 2 == p)(functools.partial(scatter, g_last, ybufs[p]))
        o_ref[...] = acc_ref[0:win * SUB, :]


def _moe(h2_slabs, ids_t, gates_t, w_gate, w_up, w_down, win):
    n_rows = ids_t.shape[1]
    n_win = n_rows // win
    d, fe = w_gate.shape[1:]
    lists, gate_tab, first, ngr = _tables(ids_t, gates_t, win)
    n_groups = lists.shape[1]
    slab = lambda w, e, first, ngr: (w, 0)
    tab = lambda w, e, first, ngr: (w, 0, 0)
    wspec = lambda w, e, first, ngr: (e, 0, 0)
    return pl.pallas_call(
        functools.partial(_moe_kernel, win),
        grid_spec=pltpu.PrefetchScalarGridSpec(
            num_scalar_prefetch=2,
            grid=(n_win, N_EXP),
            in_specs=[pl.BlockSpec((1, n_groups, LANES), tab, memory_space=pltpu.SMEM),
                      pl.BlockSpec((1, n_groups, LANES), tab),
                      pl.BlockSpec((win * SUB, LANES), slab),
                      pl.BlockSpec((1, d, fe), wspec),
                      pl.BlockSpec((1, d, fe), wspec),
                      pl.BlockSpec((1, fe, d), wspec)],
            out_specs=pl.BlockSpec((win * SUB, LANES), slab),
            scratch_shapes=[pltpu.VMEM(((win + GROUP) * SUB, LANES), F32)]
            + [pltpu.VMEM((GROUP * SUB, LANES), F32)] * 4),
        out_shape=jax.ShapeDtypeStruct((n_rows * SUB, LANES), F32),
        compiler_params=_params("arbitrary", "arbitrary"),
        name="moe_routed",
    )(first[:, :, 0].reshape(-1), ngr[:, :, 0].reshape(-1), lists, gate_tab, h2_slabs, w_gate, w_up, w_down)


def _ffn_out_kernel(r_ref, h_ref, x1_ref, mod_ref, gpost_ref, sg_ref, su_ref, sd_ref, o_ref):
    tm = x1_ref.shape[0]
    h = _load_slabs(h_ref, tm).astype(BF16)
    a = _silu(_dot(h, sg_ref[...])) * _dot(h, su_ref[...])
    y = _load_slabs(r_ref, tm) + _dot(a.astype(BF16), sd_ref[...])
    o_ref[...] = x1_ref[...] + mod_ref[0, 5:6, :] * _rms(y, gpost_ref[...])


def _ffn_out(routed, h2_slabs, x1, mod, g_post, ws_gate, ws_up, ws_down, n_batch, seq):
    n_rows, d = x1.shape
    n_lat = n_batch * seq
    tm = FFN_TILE
    fs = ws_gate.shape[-1]
    row = lambda i: (i, 0)
    const = lambda i: (0, 0)
    return pl.pallas_call(
        _ffn_out_kernel,
        grid=(n_rows // tm,),
        in_specs=[pl.BlockSpec((tm * SUB, LANES), row),
                  pl.BlockSpec((tm * SUB, LANES), row),
                  pl.BlockSpec((tm, d), row),
                  pl.BlockSpec((1, 6, d), lambda i: (_mod_row(i * tm, n_lat, seq, n_batch), 0, 0)),
                  pl.BlockSpec((1, d), const),
                  pl.BlockSpec((d, fs), const),
                  pl.BlockSpec((d, fs), const),
                  pl.BlockSpec((fs, d), const)],
        out_specs=pl.BlockSpec((tm, d), row),
        out_shape=jax.ShapeDtypeStruct((n_rows, d), F32),
        compiler_params=_params("arbitrary"),
        name="ffn_out",
    )(routed, h2_slabs, x1, mod, g_post, ws_gate, ws_up, ws_down)


def _rope_tables(seq, ctx_len):
    rows = seq // GRID_W
    row = jnp.repeat(jnp.arange(rows), GRID_W).astype(F32)
    col = jnp.tile(jnp.arange(GRID_W), rows).astype(F32)
    n_freq = HEAD_DIM // 4
    inv_freq = ROPE_THETA ** (-jnp.arange(n_freq, dtype=F32) / n_freq)
    ang = jnp.concatenate([row[:, None] * inv_freq, col[:, None] * inv_freq], axis=-1)
    reps = LANES // (HEAD_DIM // 2)
    cos = jnp.tile(jnp.cos(ang), (1, reps))
    sign = jnp.where((jnp.arange(LANES) % HEAD_DIM) < HEAD_DIM // 2, -1.0, 1.0).astype(F32)
    sin = jnp.tile(jnp.sin(ang), (1, reps)) * sign
    cos = jnp.concatenate([cos, jnp.ones((ctx_len, LANES), F32)], axis=0)
    sin = jnp.concatenate([sin, jnp.zeros((ctx_len, LANES), F32)], axis=0)
    return cos, sin


def _pad_lanes(v):
    return jnp.pad(v.reshape(1, -1), ((0, 0), (0, LANES - v.size)))


def kernel(x, c, ctx, c_ctx, w_mod, b_mod, g_pre_mix, g_post_mix, g_pre_ffn, g_post_ffn, w_in, q_norm, k_norm,
           conv_w, conv_b, dt_bias, a_log, d_skip, ssd_norm, w_out, router_w, router_bias, w_gate, w_up, w_down,
           ws_gate, ws_up, ws_down):
    n_batch, seq, d = x.shape
    ctx_len = ctx.shape[1]
    depth = w_mod.shape[0]
    n_lat = n_batch * seq
    assert n_batch < MOD_ROWS and seq % ROW_TILE == 0 and ctx_len % ROW_TILE == 0 and seq % GRID_W == 0

    xs = jnp.concatenate([x.reshape(n_lat, d), ctx.reshape(n_batch * ctx_len, d)], axis=0)
    cvec = jnp.zeros((MOD_ROWS, d), F32).at[:n_batch].set(c).at[n_batch].set(c_ctx)
    mod_all = _modulation(cvec, w_mod, b_mod)
    cos_t, sin_t = _rope_tables(seq, ctx_len)
    head_lane = jnp.arange(LANES)[:, None]
    chan_head = jnp.arange(SSD_W)[None, :] // SSD_P
    win = math.gcd(MAX_WINDOW, math.gcd(n_lat, n_batch * ctx_len))

    for i in range(depth):
        last = i == depth - 1
        mod = mod_all[i].reshape(MOD_ROWS, 6, d)
        w_in_b = jnp.pad(w_in[i].astype(BF16), ((0, 0), (0, IN_PAD - IN_W)))
        q, k, v, z, xbc, dt = _inproj(xs, mod, g_pre_mix[i].reshape(1, d), w_in_b,
                                      jnp.tile(q_norm[i], LANES // HEAD_DIM).reshape(1, LANES),
                                      jnp.tile(k_norm[i], LANES // HEAD_DIM).reshape(1, LANES),
                                      cos_t, sin_t, n_batch, seq, ctx_len)
        att = _attention(q, k, v, n_batch, seq, ctx_len, with_ctx=not last)
        xc = _conv(xbc, conv_w[i], conv_b[i].reshape(1, CONV_W), n_batch, seq, ctx_len)
        dtb = _pad_lanes(dt_bias[i])
        a_neg = _pad_lanes(-jnp.exp(a_log[i]))
        y_f = _ssd(0, xc, dt, dtb, a_neg, (head_lane == chan_head).astype(BF16), n_batch, seq, ctx_len)
        extra = (y_f, z, jnp.repeat(d_skip[i], SSD_P).reshape(1, SSD_W), ssd_norm[i].reshape(1, SSD_W))
        ssd = _ssd(1, xc, dt, dtb, a_neg, (head_lane == chan_head + SSD_HEADS).astype(BF16),
                   n_batch, seq, ctx_len, extra)
        n_rows = n_lat if last else xs.shape[0]
        x1, h2, ids_t, gates_t = _outproj(att, ssd, xs, w_out[i].astype(BF16), mod, g_post_mix[i].reshape(1, d),
                                          g_pre_ffn[i].reshape(1, d), router_w[i].T,
                                          router_bias[i].reshape(N_EXP, 1), n_rows, n_batch, seq)
        routed = _moe(h2, ids_t, gates_t, w_gate[i].astype(BF16), w_up[i].astype(BF16), w_down[i].astype(BF16), win)
        xs = _ffn_out(routed, h2, x1, mod, g_post_ffn[i].reshape(1, d), ws_gate[i].astype(BF16),
                      ws_up[i].astype(BF16), ws_down[i].astype(BF16), n_batch, seq)
    return xs.reshape(n_batch, seq, d)
```

```python
import functools
import math

import jax
import jax.numpy as jnp
from jax import lax
from jax.experimental import pallas as pl
from jax.experimental.pallas import tpu as pltpu

F32 = jnp.float32
BF16 = jnp.bfloat16

GRID_W = 64
HEADS = 8
KV_HEADS = 2
HEAD_DIM = 64
Q_PER_KV = HEADS // KV_HEADS
ATT_W = HEADS * HEAD_DIM
ROPE_THETA = 10000.0
SSD_HEADS = 8
SSD_P = 64
SSD_W = SSD_HEADS * SSD_P
SSD_G = 2
SSD_N = 128
CHUNK = 128
CONV_W = SSD_W + 2 * SSD_G * SSD_N
IN_W = ATT_W + 2 * KV_HEADS * HEAD_DIM + SSD_W + CONV_W + 2 * SSD_HEADS
LANES = 128
IN_PAD = IN_W - 2 * SSD_HEADS + LANES
N_EXP = 64
TOP_K = 8
N_GROUPS = 8
TOPK_GROUPS = 4
GROUP_SIZE = N_EXP // N_GROUPS
ROUTED_SCALE = 2.5
EPS = 1e-6
MOD_ROWS = 16
ROW_TILE = 256
MAX_WINDOW = 2048
GROUP = 128
SCATTER_BATCH = 16
RANK_CHUNK = 256
TOK_SPLIT = 64
FFN_TILE = 512
EXPERTS_PER_STEP = 4
SUB = 8
VMEM_LIMIT = 56 * 1024 * 1024
NEG_BIG = -1e30

_NT = (((1,), (1,)), ((), ()))


def _params(*sem):
    return pltpu.CompilerParams(dimension_semantics=sem, vmem_limit_bytes=VMEM_LIMIT)


def _sigmoid(v):
    return 1.0 / (1.0 + jnp.exp(-v))


def _silu(v):
    return v * _sigmoid(v)


def _rms(v, gain):
    return v * lax.rsqrt(jnp.mean(v * v, axis=-1, keepdims=True) + EPS) * gain


def _dot(a, b):
    return jnp.dot(a, b, preferred_element_type=F32)


def _split2(v):
    hi = v.astype(BF16)
    lo = (v - hi.astype(F32)).astype(BF16)
    return hi, lo


def _split3(v):
    hi = v.astype(BF16)
    r = v - hi.astype(F32)
    mid = r.astype(BF16)
    lo = (r - mid.astype(F32)).astype(BF16)
    return hi, mid, lo


def _mod_kernel(c_ref, w_ref, b_ref, o_ref):
    s = _silu(c_ref[...])
    o_ref[0] = jnp.dot(s, w_ref[0], preferred_element_type=F32, precision=lax.Precision.HIGHEST) + b_ref[0]


def _modulation(cvec, w_mod, b_mod):
    depth, d, six_d = w_mod.shape
    tn = six_d // 4
    return pl.pallas_call(
        _mod_kernel,
        grid=(depth, six_d // tn),
        in_specs=[pl.BlockSpec((MOD_ROWS, d), lambda l, j: (0, 0)),
                  pl.BlockSpec((1, d, tn), lambda l, j: (l, 0, j)),
                  pl.BlockSpec((1, 1, tn), lambda l, j: (l, 0, j))],
        out_specs=pl.BlockSpec((1, MOD_ROWS, tn), lambda l, j: (l, 0, j)),
        out_shape=jax.ShapeDtypeStruct((depth, MOD_ROWS, six_d), F32),
        compiler_params=_params("arbitrary", "arbitrary"),
        name="modulation",
    )(cvec, w_mod, b_mod.reshape(depth, 1, six_d))


def _head_norm_rope(xb, gain, cos, sin):
    lane = lax.broadcasted_iota(jnp.int32, xb.shape, 1)
    low = lane < HEAD_DIM
    sq = xb * xb
    s_lo = jnp.sum(jnp.where(low, sq, 0.0), axis=-1, keepdims=True)
    s_hi = jnp.sum(jnp.where(low, 0.0, sq), axis=-1, keepdims=True)
    ms = jnp.where(low, s_lo, s_hi) * (1.0 / HEAD_DIM)
    y = xb * lax.rsqrt(ms + EPS) * gain
    ahead = pltpu.roll(y, LANES - HEAD_DIM // 2, 1)
    behind = pltpu.roll(y, HEAD_DIM // 2, 1)
    first_half = (lane % HEAD_DIM) < (HEAD_DIM // 2)
    return y * cos + jnp.where(first_half, ahead, behind) * sin


def _inproj_kernel(x_ref, mod_ref, g_ref, w_ref, qg_ref, kg_ref, cos_ref, sin_ref,
                   q_ref, k_ref, v_ref, z_ref, xbc_ref, dt_ref):
    h = _rms(x_ref[...], g_ref[...]) * (1.0 + mod_ref[0, 1:2, :]) + mod_ref[0, 0:1, :]
    hb = h.astype(BF16)
    cos = cos_ref[...]
    sin = sin_ref[...]
    c0 = 0
    for blk in range(ATT_W // LANES):
        qb = _dot(hb, w_ref[:, c0:c0 + LANES])
        qb = _head_norm_rope(qb, qg_ref[...], cos, sin) * (HEAD_DIM ** -0.5)
        q_ref[:, c0:c0 + LANES] = qb.astype(q_ref.dtype)
        c0 += LANES
    kb = _head_norm_rope(_dot(hb, w_ref[:, c0:c0 + LANES]), kg_ref[...], cos, sin)
    c0 += LANES
    vb = _dot(hb, w_ref[:, c0:c0 + LANES])
    c0 += LANES
    for g in range(KV_HEADS):
        k_ref[g] = kb[:, g * HEAD_DIM:(g + 1) * HEAD_DIM].astype(k_ref.dtype)
        v_ref[g] = vb[:, g * HEAD_DIM:(g + 1) * HEAD_DIM].astype(v_ref.dtype)
    z_ref[...] = _dot(hb, w_ref[:, c0:c0 + SSD_W])
    c0 += SSD_W
    xbc_ref[...] = _dot(hb, w_ref[:, c0:c0 + CONV_W])
    c0 += CONV_W
    dt_ref[...] = _dot(hb, w_ref[:, c0:c0 + LANES])


def _mod_row(start_row, n_lat, seq, n_batch):
    return jnp.where(start_row < n_lat, start_row // seq, n_batch)


def _inproj(xs, mod, g_pre, w_in, q_gain, k_gain, cos_t, sin_t, n_batch, seq, ctx_len):
    n, d = xs.shape
    n_lat = n_batch * seq
    tm = ROW_TILE
    lat_tiles = seq // tm
    ctx_tiles = ctx_len // tm

    def tab_idx(i):
        return (jnp.where(i * tm < n_lat, i % lat_tiles, lat_tiles + (i - n_lat // tm) % ctx_tiles), 0)

    row = lambda i: (i, 0)
    const = lambda i: (0, 0)
    return pl.pallas_call(
        _inproj_kernel,
        grid=(n // tm,),
        in_specs=[pl.BlockSpec((tm, d), row),
                  pl.BlockSpec((1, 6, d), lambda i: (_mod_row(i * tm, n_lat, seq, n_batch), 0, 0)),
                  pl.BlockSpec((1, d), const),
                  pl.BlockSpec((d, IN_PAD), const),
                  pl.BlockSpec((1, LANES), const),
                  pl.BlockSpec((1, LANES), const),
                  pl.BlockSpec((tm, LANES), tab_idx),
                  pl.BlockSpec((tm, LANES), tab_idx)],
        out_specs=[pl.BlockSpec((tm, ATT_W), row),
                   pl.BlockSpec((KV_HEADS, tm, HEAD_DIM), lambda i: (0, i, 0)),
                   pl.BlockSpec((KV_HEADS, tm, HEAD_DIM), lambda i: (0, i, 0)),
                   pl.BlockSpec((tm, SSD_W), row),
                   pl.BlockSpec((tm, CONV_W), row),
                   pl.BlockSpec((tm, LANES), row)],
        out_shape=[jax.ShapeDtypeStruct((n, ATT_W), BF16),
                   jax.ShapeDtypeStruct((KV_HEADS, n, HEAD_DIM), BF16),
                   jax.ShapeDtypeStruct((KV_HEADS, n, HEAD_DIM), BF16),
                   jax.ShapeDtypeStruct((n, SSD_W), F32),
                   jax.ShapeDtypeStruct((n, CONV_W), F32),
                   jax.ShapeDtypeStruct((n, LANES), F32)],
        compiler_params=_params("arbitrary"),
        name="inproj",
    )(xs, mod, g_pre, w_in, q_gain, k_gain, cos_t, sin_t)


def _attn_heads(q_ref, kc_ref, vc_ref, kl_ref, vl_ref, o_ref, latent):
    outs = []
    q = q_ref[...]
    for h in range(HEADS):
        g = h // Q_PER_KV
        qh = q[:, h * HEAD_DIM:(h + 1) * HEAD_DIM]
        sc = lax.dot_general(qh, kc_ref[g], _NT, preferred_element_type=F32)
        m = jnp.max(sc, axis=-1, keepdims=True)
        if latent:
            sl = lax.dot_general(qh, kl_ref[g], _NT, preferred_element_type=F32)
            m = jnp.maximum(m, jnp.max(sl, axis=-1, keepdims=True))
            pw = jnp.exp(sl - m)
        pc = jnp.exp(sc - m)
        den = jnp.sum(pc, axis=-1, keepdims=True)
        acc = _dot(pc.astype(BF16), vc_ref[g])
        if latent:
            den = den + jnp.sum(pw, axis=-1, keepdims=True)
            acc = acc + _dot(pw.astype(BF16), vl_ref[g])
        outs.append(acc / den)
    o_ref[...] = jnp.concatenate(outs, axis=-1).astype(o_ref.dtype)


def _attn_kernel(n_lat_q, with_ctx, q_ref, kc_ref, vc_ref, kl_ref, vl_ref, o_ref):
    if not with_ctx:
        _attn_heads(q_ref, kc_ref, vc_ref, kl_ref, vl_ref, o_ref, True)
        return
    j = pl.program_id(1)

    @pl.when(j < n_lat_q)
    def _():
        _attn_heads(q_ref, kc_ref, vc_ref, kl_ref, vl_ref, o_ref, True)

    @pl.when(j >= n_lat_q)
    def _():
        _attn_heads(q_ref, kc_ref, vc_ref, kl_ref, vl_ref, o_ref, False)


def _attention(q, k, v, n_batch, seq, ctx_len, with_ctx):
    n = q.shape[0]
    n_lat = n_batch * seq
    tq = ROW_TILE
    n_lat_q = seq // tq
    n_ctx_q = ctx_len // tq if with_ctx else 0

    def q_idx(b, j):
        return (jnp.where(j < n_lat_q, b * n_lat_q + j, n_lat // tq + b * (ctx_len // tq) + (j - n_lat_q)), 0)

    ctx_idx = lambda b, j: (0, n_lat // ctx_len + b, 0)
    lat_idx = lambda b, j: (0, b, 0)
    return pl.pallas_call(
        functools.partial(_attn_kernel, n_lat_q, with_ctx),
        grid=(n_batch, n_lat_q + n_ctx_q),
        in_specs=[pl.BlockSpec((tq, ATT_W), q_idx),
                  pl.BlockSpec((KV_HEADS, ctx_len, HEAD_DIM), ctx_idx),
                  pl.BlockSpec((KV_HEADS, ctx_len, HEAD_DIM), ctx_idx),
                  pl.BlockSpec((KV_HEADS, seq, HEAD_DIM), lat_idx),
                  pl.BlockSpec((KV_HEADS, seq, HEAD_DIM), lat_idx)],
        out_specs=pl.BlockSpec((tq, ATT_W), q_idx),
        out_shape=jax.ShapeDtypeStruct((n if with_ctx else n_lat, ATT_W), BF16),
        compiler_params=_params("arbitrary", "arbitrary"),
        name="attention",
    )(q, k, v, k, v)


def _conv_chunk(ncc, nlc, u_ref, p_ref, nx_ref, w_ref, b_ref):
    j = pl.program_id(1)
    pos = jnp.where(j < ncc, j, j - ncc)
    last_pos = jnp.where(j < ncc, ncc - 1, nlc - 1)
    u = u_ref[...]
    r = lax.broadcasted_iota(jnp.int32, u.shape, 0)
    prev_row = jnp.where(pos == 0, 0.0, p_ref[SUB - 1:SUB, :])
    next_row = jnp.where(pos == last_pos, 0.0, nx_ref[0:1, :])
    up = jnp.where(r == 0, prev_row, pltpu.roll(u, 1, 0))
    dn = jnp.where(r == CHUNK - 1, next_row, pltpu.roll(u, CHUNK - 1, 0))
    return _silu(w_ref[0:1, :] * up + w_ref[1:2, :] * u + w_ref[2:3, :] * dn + b_ref[...])


def _ssd_kernel(direction, ncc, nlc, *refs):
    if direction == 0:
        (u_ref, p_ref, nx_ref, cw_ref, cb_ref, dt_ref, dtb_ref, a_ref, ex_ref, y_ref, xc_ref, st_ref) = refs
        xc = _conv_chunk(ncc, nlc, u_ref, p_ref, nx_ref, cw_ref, cb_ref)
        xc_ref[...] = xc
    else:
        (xc_in_ref, dt_ref, dtb_ref, a_ref, ex_ref, yf_ref, z_ref, dsk_ref, gn_ref, y_ref, st_ref) = refs
        xc = xc_in_ref[...]

    @pl.when(pl.program_id(1) == 0)
    def _():
        st_ref[...] = jnp.zeros_like(st_ref)

    xs = xc[:, 0:SSD_W]
    b_all = xc[:, SSD_W:SSD_W + SSD_G * SSD_N]
    c_all = xc[:, SSD_W + SSD_G * SSD_N:]
    pre = dt_ref[...] + dtb_ref[...]
    dtv = jnp.maximum(pre, 0.0) + jnp.log1p(jnp.exp(-jnp.abs(pre)))
    la = dtv * a_ref[...]
    row = lax.broadcasted_iota(jnp.int32, (CHUNK, CHUNK), 0)
    col = lax.broadcasted_iota(jnp.int32, (CHUNK, CHUNK), 1)
    tri = (col <= row) if direction == 0 else (col >= row)
    tri_b = jnp.where(tri, 1.0, 0.0).astype(BF16)
    cs = sum(_dot(tri_b, part) for part in _split3(la))
    ex = ex_ref[...]

    def expand(val):
        return sum(_dot(part, ex) for part in _split2(val))

    edge = CHUNK - 1 if direction == 0 else 0
    tot = cs[edge:edge + 1, :]
    dt_e = expand(dtv)
    da_e = expand(jnp.exp(cs))
    db_e = expand(jnp.exp(tot - cs))
    dtot_e = da_e[edge:edge + 1, :]
    xd = xs * dt_e
    xd_end = (xd * db_e).astype(BF16)
    cs_t = cs.T
    st = st_ref[...]
    st_b = st.astype(BF16)
    ys = []
    new_st = []
    gw = SSD_W // SSD_G
    hpg = SSD_HEADS // SSD_G
    for g in range(SSD_G):
        bg = b_all[:, g * SSD_N:(g + 1) * SSD_N]
        cg = c_all[:, g * SSD_N:(g + 1) * SSD_N].astype(BF16)
        gram = lax.dot_general(cg, bg.astype(BF16), _NT, preferred_element_type=F32)
        y_off = _dot(cg, st_b[:, g * gw:(g + 1) * gw]) * da_e[:, g * gw:(g + 1) * gw]
        new_st.append(st[:, g * gw:(g + 1) * gw] * dtot_e[:, g * gw:(g + 1) * gw]
                      + _dot(bg.T.astype(BF16), xd_end[:, g * gw:(g + 1) * gw]))
        for hh in range(hpg):
            h = g * hpg + hh
            c = direction * SSD_HEADS + h
            diff = cs[:, c:c + 1] - cs_t[c:c + 1, :]
            decay = jnp.exp(jnp.where(tri, diff, NEG_BIG))
            m = (gram * decay).astype(BF16)
            y_d = _dot(m, xd[:, h * SSD_P:(h + 1) * SSD_P].astype(BF16))
            ys.append(y_d + y_off[:, hh * SSD_P:(hh + 1) * SSD_P])
    st_ref[...] = jnp.concatenate(new_st, axis=-1)
    y = jnp.concatenate(ys, axis=-1)
    if direction == 0:
        y_ref[...] = y
    else:
        z = z_ref[...]
        gated = (yf_ref[...] + y + xs * dsk_ref[...]) * _silu(z)
        y_ref[...] = _rms(gated, gn_ref[...]).astype(y_ref.dtype)


def _ssd(direction, xin, dt, dtb, a_neg, ex, n_batch, seq, ctx_len, extra):
    n, cw = xin.shape
    n_lat = n_batch * seq
    ncc = ctx_len // CHUNK
    nlc = seq // CHUNK
    per = CHUNK // SUB

    def rows(b, j):
        if direction == 0:
            return jnp.where(j < ncc, n_lat // CHUNK + b * ncc + j, b * nlc + (j - ncc))
        return jnp.where(j < ncc, n_lat // CHUNK + b * ncc + (ncc - 1 - j), b * nlc + (nlc - 1 - (j - ncc)))

    const = lambda b, j: (0, 0)
    chunk = lambda width: pl.BlockSpec((CHUNK, width), lambda b, j: (rows(b, j), 0))
    scan_specs = [chunk(LANES), pl.BlockSpec((1, LANES), const), pl.BlockSpec((1, LANES), const),
                  pl.BlockSpec((LANES, SSD_W), const)]
    if direction == 0:
        conv_w, conv_b = extra
        in_specs = [chunk(cw),
                    pl.BlockSpec((SUB, cw), lambda b, j: (jnp.maximum(rows(b, j) * per - 1, 0), 0)),
                    pl.BlockSpec((SUB, cw), lambda b, j: (jnp.minimum((rows(b, j) + 1) * per, n // SUB - 1), 0)),
                    pl.BlockSpec((3, cw), const), pl.BlockSpec((1, cw), const)] + scan_specs
        args = [xin, xin, xin, conv_w, conv_b, dt, dtb, a_neg, ex]
        out_specs = [chunk(SSD_W), chunk(cw)]
        out_shape = [jax.ShapeDtypeStruct((n, SSD_W), F32), jax.ShapeDtypeStruct((n, cw), F32)]
    else:
        yf, z, dskip, gnorm = extra
        in_specs = [chunk(cw)] + scan_specs + [chunk(SSD_W), chunk(SSD_W), pl.BlockSpec((1, SSD_W), const),
                                               pl.BlockSpec((1, SSD_W), const)]
        args = [xin, dt, dtb, a_neg, ex, yf, z, dskip, gnorm]
        out_specs = chunk(SSD_W)
        out_shape = jax.ShapeDtypeStruct((n, SSD_W), BF16)
    return pl.pallas_call(
        functools.partial(_ssd_kernel, direction, ncc, nlc),
        grid=(n_batch, ncc + nlc),
        in_specs=in_specs,
        out_specs=out_specs,
        out_shape=out_shape,
        scratch_shapes=[pltpu.VMEM((SSD_N, SSD_W), F32)],
        compiler_params=_params("arbitrary", "arbitrary"),
        name="ssd_fwd" if direction == 0 else "ssd_bwd",
    )(*args)


def _route(logits, bias):
    tokens = logits.shape[1]
    scores = _sigmoid(logits)
    choice = scores + bias
    sub = lax.broadcasted_iota(jnp.int32, (GROUP_SIZE, tokens), 0).astype(F32)
    neg = -jnp.inf

    def pick_first_max(v, idx, sentinel):
        m = jnp.max(v, axis=0, keepdims=True)
        first = jnp.min(jnp.where(v == m, idx, sentinel), axis=0, keepdims=True)
        return m, idx == first

    group_scores = []
    for g in range(N_GROUPS):
        cg = choice[g * GROUP_SIZE:(g + 1) * GROUP_SIZE, :]
        m1, hit = pick_first_max(cg, sub, GROUP_SIZE)
        m2 = jnp.max(jnp.where(hit, neg, cg), axis=0, keepdims=True)
        group_scores.append(m1 + m2)
    v = jnp.concatenate(group_scores, axis=0)
    gsel = jnp.zeros_like(v)
    for _ in range(TOPK_GROUPS):
        _, hit = pick_first_max(v, sub, N_GROUPS)
        gsel = jnp.where(hit, 1.0, gsel)
        v = jnp.where(hit, neg, v)
    ok = jnp.concatenate([jnp.broadcast_to(gsel[g:g + 1, :], (GROUP_SIZE, tokens)) for g in range(N_GROUPS)],
                         axis=0) > 0.5
    v = jnp.where(ok, choice, neg)
    eidx = lax.broadcasted_iota(jnp.int32, (N_EXP, tokens), 0).astype(F32)
    ids = []
    gates = []
    for _ in range(TOP_K):
        _, hit = pick_first_max(v, eidx, N_EXP)
        ids.append(jnp.sum(jnp.where(hit, eidx, 0.0), axis=0, keepdims=True))
        gates.append(jnp.sum(jnp.where(hit, scores, 0.0), axis=0, keepdims=True))
        v = jnp.where(hit, neg, v)
    ids = jnp.concatenate(ids, axis=0)
    gates = jnp.concatenate(gates, axis=0)
    gates = gates / jnp.sum(gates, axis=0, keepdims=True) * ROUTED_SCALE
    return ids.astype(jnp.int32), gates


def _store_slabs(slab_ref, rows):
    for kc in range(SUB):
        slab_ref[pl.ds(kc, rows.shape[0], stride=SUB), :] = rows[:, kc * LANES:(kc + 1) * LANES]


def _load_slabs(slab_ref, n):
    return jnp.concatenate([slab_ref[pl.ds(kc, n, stride=SUB), :] for kc in range(SUB)], axis=1)


def _outproj_kernel(att_ref, ssd_ref, x_ref, wo_ref, mod_ref, gpost_ref, gpre_ref, wrh_ref, wrl_ref, rb_ref,
                    x1_ref, h2_ref, ids_ref, gates_ref):
    m = _dot(att_ref[...], wo_ref[0:ATT_W, :]) + _dot(ssd_ref[...], wo_ref[ATT_W:, :])
    x1 = x_ref[...] + mod_ref[0, 2:3, :] * _rms(m, gpost_ref[...])
    x1_ref[...] = x1
    h2 = _rms(x1, gpre_ref[...]) * (1.0 + mod_ref[0, 4:5, :]) + mod_ref[0, 3:4, :]
    _store_slabs(h2_ref, h2)
    h_hi, h_lo = _split2(h2)
    nt = lambda a, b: lax.dot_general(a, b, _NT, preferred_element_type=F32)
    logits = nt(wrh_ref[...], h_hi) + (nt(wrh_ref[...], h_lo) + nt(wrl_ref[...], h_hi))
    ids_ref[...], gates_ref[...] = _route(logits, rb_ref[...])


def _outproj(att, ssd, xs, w_out, mod, g_post, g_pre, wr_t, r_bias, n_rows, n_batch, seq):
    d = xs.shape[1]
    n_lat = n_batch * seq
    tm = FFN_TILE
    wr_hi = wr_t.astype(BF16)
    wr_lo = (wr_t - wr_hi.astype(F32)).astype(BF16)
    row = lambda i: (i, 0)
    const = lambda i: (0, 0)
    return pl.pallas_call(
        _outproj_kernel,
        grid=(n_rows // tm,),
        in_specs=[pl.BlockSpec((tm, ATT_W), row),
                  pl.BlockSpec((tm, SSD_W), row),
                  pl.BlockSpec((tm, d), row),
                  pl.BlockSpec((ATT_W + SSD_W, d), const),
                  pl.BlockSpec((1, 6, d), lambda i: (_mod_row(i * tm, n_lat, seq, n_batch), 0, 0)),
                  pl.BlockSpec((1, d), const),
                  pl.BlockSpec((1, d), const),
                  pl.BlockSpec((N_EXP, d), const),
                  pl.BlockSpec((N_EXP, d), const),
                  pl.BlockSpec((N_EXP, 1), const)],
        out_specs=[pl.BlockSpec((tm, d), row),
                   pl.BlockSpec((tm * SUB, LANES), row),
                   pl.BlockSpec((TOP_K, tm), lambda i: (0, i)),
                   pl.BlockSpec((TOP_K, tm), lambda i: (0, i))],
        out_shape=[jax.ShapeDtypeStruct((n_rows, d), F32),
                   jax.ShapeDtypeStruct((n_rows * SUB, LANES), F32),
                   jax.ShapeDtypeStruct((TOP_K, n_rows), jnp.int32),
                   jax.ShapeDtypeStruct((TOP_K, n_rows), F32)],
        compiler_params=_params("arbitrary"),
        name="outproj_router",
    )(att, ssd, xs, w_out, mod, g_post, g_pre, wr_hi, wr_lo, r_bias)


def _tables_kernel(win, n_groups, ids_ref, gates_ref, list_ref, gl_ref, first_ref, ngr_ref):
    ids = ids_ref[...]
    gts = gates_ref[...]
    eidx = lax.broadcasted_iota(jnp.int32, (N_EXP, win), 0)
    hits = [ids[k:k + 1, :] == eidx for k in range(TOP_K)]
    sel = sum(jnp.where(h, 1.0, 0.0) for h in hits)
    cnt = jnp.sum(sel, axis=1, keepdims=True)
    ngr = jnp.floor((cnt + (GROUP - 1)) * (1.0 / GROUP))
    er = lax.broadcasted_iota(jnp.int32, (N_EXP, N_EXP), 0)
    ec = lax.broadcasted_iota(jnp.int32, (N_EXP, N_EXP), 1)
    below = jnp.where(ec < er, 1.0, 0.0).astype(BF16)
    ngr_b = jnp.broadcast_to(ngr, (N_EXP, LANES))
    first = _dot(below, ngr_b.astype(BF16)) + 1.0
    cr = lax.broadcasted_iota(jnp.int32, (RANK_CHUNK, RANK_CHUNK), 0)
    cc = lax.broadcasted_iota(jnp.int32, (RANK_CHUNK, RANK_CHUNK), 1)
    before = jnp.where(cr < cc, 1.0, 0.0).astype(BF16)
    carry = jnp.zeros((N_EXP, 1), F32)
    ranks = []
    for c0 in range(0, win, RANK_CHUNK):
        s = sel[:, c0:c0 + RANK_CHUNK]
        ranks.append(_dot(s.astype(BF16), before) + carry)
        carry = carry + jnp.sum(s, axis=1, keepdims=True)
    slot_of = first[:, 0:1] * GROUP + jnp.concatenate(ranks, axis=1)
    slot = jnp.concatenate([jnp.sum(jnp.where(h, slot_of, 0.0), axis=0, keepdims=True) for h in hits], axis=0)
    hi = jnp.floor(slot * (1.0 / GROUP))
    lo = slot - hi * GROUP
    g1 = gts.astype(BF16).astype(F32)
    rest = gts - g1
    g2 = rest.astype(BF16).astype(F32)
    g3 = rest - g2
    stack = jnp.concatenate([lo, g1, g2, g3, jnp.zeros((LANES - 4 * TOP_K, win), F32)], axis=0)
    cols = stack.T
    tok1 = lax.broadcasted_iota(jnp.int32, (win, LANES), 0) + 1
    tok_a = (tok1 // TOK_SPLIT).astype(F32)
    tok_b = (tok1 % TOK_SPLIT).astype(F32)
    lane = lax.broadcasted_iota(jnp.int32, (win, LANES), 1).astype(F32)
    gidx = lax.broadcasted_iota(jnp.int32, (n_groups, win), 0).astype(F32)
    acc = jnp.zeros((n_groups, 5 * LANES), F32)
    for k in range(TOP_K):
        onehot = cols[:, k:k + 1] == lane
        vals = (tok_a, tok_b, cols[:, TOP_K + k:TOP_K + k + 1], cols[:, 2 * TOP_K + k:2 * TOP_K + k + 1],
                cols[:, 3 * TOP_K + k:3 * TOP_K + k + 1])
        rhs = jnp.concatenate([jnp.where(onehot, v, 0.0) for v in vals], axis=1).astype(BF16)
        lhs = jnp.where(hi[k:k + 1, :] == gidx, 1.0, 0.0).astype(BF16)
        acc = acc + _dot(lhs, rhs)
    tok1_tab = acc[:, 0:LANES] * TOK_SPLIT + acc[:, LANES:2 * LANES]
    gate_tab = acc[:, 2 * LANES:3 * LANES] + acc[:, 3 * LANES:4 * LANES] + acc[:, 4 * LANES:5 * LANES]
    spare = lax.broadcasted_iota(jnp.int32, (n_groups, LANES), 1).astype(F32) + win
    row_of = jnp.where(tok1_tab > 0.5, tok1_tab - 1.0, spare) * SUB
    list_ref[0] = row_of.astype(jnp.int32)
    gl_ref[0] = gate_tab
    first_ref[0] = first.astype(jnp.int32)
    ngr_ref[0] = ngr_b.astype(jnp.int32)


def _tables(ids_t, gates_t, win):
    n = ids_t.shape[1]
    n_win = n // win
    n_groups = win * TOP_K // GROUP + N_EXP + SUB
    blk = lambda w: (0, w)
    out = lambda w: (w, 0, 0)
    return pl.pallas_call(
        functools.partial(_tables_kernel, win, n_groups),
        grid=(n_win,),
        in_specs=[pl.BlockSpec((TOP_K, win), blk), pl.BlockSpec((TOP_K, win), blk)],
        out_specs=[pl.BlockSpec((1, n_groups, LANES), out), pl.BlockSpec((1, n_groups, LANES), out),
                   pl.BlockSpec((1, N_EXP, LANES), out), pl.BlockSpec((1, N_EXP, LANES), out)],
        out_shape=[jax.ShapeDtypeStruct((n_win, n_groups, LANES), jnp.int32),
                   jax.ShapeDtypeStruct((n_win, n_groups, LANES), F32),
                   jax.ShapeDtypeStruct((n_win, N_EXP, LANES), jnp.int32),
                   jax.ShapeDtypeStruct((n_win, N_EXP, LANES), jnp.int32)],
        compiler_params=_params("arbitrary"),
        name="dispatch_tables",
    )(ids_t, gates_t)


def _moe_kernel(win, first_ref, ngr_ref, lst_ref, gl_ref, src_ref, wg_ref, wu_ref, wd_ref, o_ref,
                buf0_ref, buf1_ref, ybuf0_ref, ybuf1_ref):
    w = pl.program_id(0)
    step = pl.program_id(1)
    acc_ref = o_ref.at[0]
    last_row = (win - 1) * SUB
    bufs = (buf0_ref, buf1_ref)
    ybufs = (ybuf0_ref, ybuf1_ref)

    def gather(g, buf_ref):
        for r in range(GROUP):
            row = pl.multiple_of(jnp.minimum(lst_ref[0, g, r], last_row), SUB)
            buf_ref[pl.ds(r * SUB, SUB), :] = src_ref[pl.ds(row, SUB), :]

    def scatter(g, ybuf_ref):
        for r0 in range(0, GROUP, SCATTER_BATCH):
            rows = [pl.multiple_of(lst_ref[0, g, r0 + u], SUB) for u in range(SCATTER_BATCH)]
            vals = [acc_ref[pl.ds(rows[u], SUB), :] + ybuf_ref[pl.ds((r0 + u) * SUB, SUB), :]
                    for u in range(SCATTER_BATCH)]
            for u in range(SCATTER_BATCH):
                acc_ref[pl.ds(rows[u], SUB), :] = vals[u]

    def expert(g, k, buf_ref, ybuf_ref):
        x = _load_slabs(buf_ref, GROUP).astype(BF16)
        a = _silu(_dot(x, wg_ref[k])) * _dot(x, wu_ref[k])
        gate_row = jnp.broadcast_to(gl_ref[0, pl.ds(g, 1), :], (GROUP, LANES))
        diag = (lax.broadcasted_iota(jnp.int32, (GROUP, LANES), 0)
                == lax.broadcasted_iota(jnp.int32, (GROUP, LANES), 1))
        gate_col = jnp.sum(jnp.where(diag, gate_row, 0.0), axis=1, keepdims=True)
        _store_slabs(ybuf_ref, _dot((a * gate_col).astype(BF16), wd_ref[k]))

    def stage(g, k, p):
        gather(g + 1, bufs[1 - p])
        expert(g, k, bufs[p], ybufs[p])
        scatter(g - 1, ybufs[1 - p])

    @pl.when(step == 0)
    def _():
        o_ref[...] = jnp.zeros_like(o_ref)
        ybuf0_ref[...] = jnp.zeros_like(ybuf0_ref)
        gather(1, bufs[1])

    g_end = 0
    for k in range(EXPERTS_PER_STEP):
        e = w * N_EXP + step * EXPERTS_PER_STEP + k
        g0 = first_ref[e]
        n = ngr_ref[e]
        g_end = g0 + n

        def body(i, carry, g0=g0, k=k):
            g = g0 + i
            for p in range(2):
                pl.when(g % 2 == p)(functools.partial(stage, g, k, p))
            return carry

        lax.fori_loop(0, n, body, 0)

    @pl.when(step == pl.num_programs(1) - 1)
    def _():
        for p in range(2):
            pl.when((g_end - 1) % 2 == p)(functools.partial(scatter, g_end - 1, ybufs[p]))


def _moe(h2_slabs, ids_t, gates_t, w_gate, w_up, w_down, win):
    n_rows = ids_t.shape[1]
    n_win = n_rows // win
    d, fe = w_gate.shape[1:]
    lists, gate_tab, first, ngr = _tables(ids_t, gates_t, win)
    n_groups = lists.shape[1]
    slab = lambda w, s, first, ngr: (w, 0)
    tab = lambda w, s, first, ngr: (w, 0, 0)
    wspec = lambda w, s, first, ngr: (s, 0, 0)
    eps = EXPERTS_PER_STEP
    return pl.pallas_call(
        functools.partial(_moe_kernel, win),
        grid_spec=pltpu.PrefetchScalarGridSpec(
            num_scalar_prefetch=2,
            grid=(n_win, N_EXP // eps),
            in_specs=[pl.BlockSpec((1, n_groups, LANES), tab, memory_space=pltpu.SMEM),
                      pl.BlockSpec((1, n_groups, LANES), tab),
                      pl.BlockSpec((win * SUB, LANES), slab),
                      pl.BlockSpec((eps, d, fe), wspec),
                      pl.BlockSpec((eps, d, fe), wspec),
                      pl.BlockSpec((eps, fe, d), wspec)],
            out_specs=pl.BlockSpec((1, (win + GROUP) * SUB, LANES), tab),
            scratch_shapes=[pltpu.VMEM((GROUP * SUB, LANES), F32)] * 4),
        out_shape=jax.ShapeDtypeStruct((n_win, (win + GROUP) * SUB, LANES), F32),
        compiler_params=_params("arbitrary", "arbitrary"),
        name="moe_routed",
    )(first[:, :, 0].reshape(-1), ngr[:, :, 0].reshape(-1), lists, gate_tab, h2_slabs, w_gate, w_up, w_down)


def _ffn_out_kernel(r_ref, h_ref, x1_ref, mod_ref, gpost_ref, sg_ref, su_ref, sd_ref, o_ref):
    tm = x1_ref.shape[0]
    h = _load_slabs(h_ref, tm).astype(BF16)
    a = _silu(_dot(h, sg_ref[...])) * _dot(h, su_ref[...])
    y = _load_slabs(r_ref.at[0], tm) + _dot(a.astype(BF16), sd_ref[...])
    o_ref[...] = x1_ref[...] + mod_ref[0, 5:6, :] * _rms(y, gpost_ref[...])


def _ffn_out(routed, h2_slabs, x1, mod, g_post, ws_gate, ws_up, ws_down, n_batch, seq, win):
    n_rows, d = x1.shape
    n_lat = n_batch * seq
    tm = FFN_TILE
    per_win = win // tm
    fs = ws_gate.shape[-1]
    row = lambda i: (i, 0)
    const = lambda i: (0, 0)
    return pl.pallas_call(
        _ffn_out_kernel,
        grid=(n_rows // tm,),
        in_specs=[pl.BlockSpec((1, tm * SUB, LANES), lambda i: (i // per_win, i % per_win, 0)),
                  pl.BlockSpec((tm * SUB, LANES), row),
                  pl.BlockSpec((tm, d), row),
                  pl.BlockSpec((1, 6, d), lambda i: (_mod_row(i * tm, n_lat, seq, n_batch), 0, 0)),
                  pl.BlockSpec((1, d), const),
                  pl.BlockSpec((d, fs), const),
                  pl.BlockSpec((d, fs), const),
                  pl.BlockSpec((fs, d), const)],
        out_specs=pl.BlockSpec((tm, d), row),
        out_shape=jax.ShapeDtypeStruct((n_rows, d), F32),
        compiler_params=_params("arbitrary"),
        name="ffn_out",
    )(routed, h2_slabs, x1, mod, g_post, ws_gate, ws_up, ws_down)


def _rope_tables(seq, ctx_len):
    rows = seq // GRID_W
    row = jnp.repeat(jnp.arange(rows), GRID_W).astype(F32)
    col = jnp.tile(jnp.arange(GRID_W), rows).astype(F32)
    n_freq = HEAD_DIM // 4
    inv_freq = ROPE_THETA ** (-jnp.arange(n_freq, dtype=F32) / n_freq)
    ang = jnp.concatenate([row[:, None] * inv_freq, col[:, None] * inv_freq], axis=-1)
    reps = LANES // (HEAD_DIM // 2)
    cos = jnp.tile(jnp.cos(ang), (1, reps))
    sign = jnp.where((jnp.arange(LANES) % HEAD_DIM) < HEAD_DIM // 2, -1.0, 1.0).astype(F32)
    sin = jnp.tile(jnp.sin(ang), (1, reps)) * sign
    cos = jnp.concatenate([cos, jnp.ones((ctx_len, LANES), F32)], axis=0)
    sin = jnp.concatenate([sin, jnp.zeros((ctx_len, LANES), F32)], axis=0)
    return cos, sin


def _pad_lanes(v):
    return jnp.pad(v.reshape(1, -1), ((0, 0), (0, LANES - v.size)))


def kernel(x, c, ctx, c_ctx, w_mod, b_mod, g_pre_mix, g_post_mix, g_pre_ffn, g_post_ffn, w_in, q_norm, k_norm,
           conv_w, conv_b, dt_bias, a_log, d_skip, ssd_norm, w_out, router_w, router_bias, w_gate, w_up, w_down,
           ws_gate, ws_up, ws_down):
    n_batch, seq, d = x.shape
    ctx_len = ctx.shape[1]
    depth = w_mod.shape[0]
    n_lat = n_batch * seq
    assert n_batch < MOD_ROWS and seq % ROW_TILE == 0 and ctx_len % ROW_TILE == 0 and seq % GRID_W == 0

    xs = jnp.concatenate([x.reshape(n_lat, d), ctx.reshape(n_batch * ctx_len, d)], axis=0)
    cvec = jnp.zeros((MOD_ROWS, d), F32).at[:n_batch].set(c).at[n_batch].set(c_ctx)
    mod_all = _modulation(cvec, w_mod, b_mod)
    cos_t, sin_t = _rope_tables(seq, ctx_len)
    head_lane = jnp.arange(LANES)[:, None]
    chan_head = jnp.arange(SSD_W)[None, :] // SSD_P
    win = math.gcd(MAX_WINDOW, math.gcd(n_lat, n_batch * ctx_len))

    for i in range(depth):
        last = i == depth - 1
        mod = mod_all[i].reshape(MOD_ROWS, 6, d)
        w_in_b = jnp.pad(w_in[i].astype(BF16), ((0, 0), (0, IN_PAD - IN_W)))
        q, k, v, z, xbc, dt = _inproj(xs, mod, g_pre_mix[i].reshape(1, d), w_in_b,
                                      jnp.tile(q_norm[i], LANES // HEAD_DIM).reshape(1, LANES),
                                      jnp.tile(k_norm[i], LANES // HEAD_DIM).reshape(1, LANES),
                                      cos_t, sin_t, n_batch, seq, ctx_len)
        att = _attention(q, k, v, n_batch, seq, ctx_len, with_ctx=not last)
        dtb = _pad_lanes(dt_bias[i])
        a_neg = _pad_lanes(-jnp.exp(a_log[i]))
        y_f, xc = _ssd(0, xbc, dt, dtb, a_neg, (head_lane == chan_head).astype(BF16), n_batch, seq, ctx_len,
                       (conv_w[i], conv_b[i].reshape(1, CONV_W)))
        extra = (y_f, z, jnp.repeat(d_skip[i], SSD_P).reshape(1, SSD_W), ssd_norm[i].reshape(1, SSD_W))
        ssd = _ssd(1, xc, dt, dtb, a_neg, (head_lane == chan_head + SSD_HEADS).astype(BF16),
                   n_batch, seq, ctx_len, extra)
        n_rows = n_lat if last else xs.shape[0]
        x1, h2, ids_t, gates_t = _outproj(att, ssd, xs, w_out[i].astype(BF16), mod, g_post_mix[i].reshape(1, d),
                                          g_pre_ffn[i].reshape(1, d), router_w[i].T,
                                          router_bias[i].reshape(N_EXP, 1), n_rows, n_batch, seq)
        routed = _moe(h2, ids_t, gates_t, w_gate[i].astype(BF16), w_up[i].astype(BF16), w_down[i].astype(BF16), win)
        xs = _ffn_out(routed, h2, x1, mod, g_post_ffn[i].reshape(1, d), ws_gate[i].astype(BF16),
                      ws_up[i].astype(BF16), ws_down[i].astype(BF16), n_batch, seq, win)
    return xs.reshape(n_batch, seq, d)
```

```python
import functools
import math

import jax
import jax.numpy as jnp
from jax import lax
from jax.experimental import pallas as pl
from jax.experimental.pallas import tpu as pltpu

F32 = jnp.float32
BF16 = jnp.bfloat16

GRID_W = 64
HEADS = 8
KV_HEADS = 2
HEAD_DIM = 64
Q_PER_KV = HEADS // KV_HEADS
ATT_W = HEADS * HEAD_DIM
ROPE_THETA = 10000.0
SSD_HEADS = 8
SSD_P = 64
SSD_W = SSD_HEADS * SSD_P
SSD_G = 2
SSD_N = 128
CHUNK = 128
CONV_W = SSD_W + 2 * SSD_G * SSD_N
IN_W = ATT_W + 2 * KV_HEADS * HEAD_DIM + SSD_W + CONV_W + 2 * SSD_HEADS
LANES = 128
IN_PAD = IN_W - 2 * SSD_HEADS + LANES
N_EXP = 64
TOP_K = 8
N_GROUPS = 8
TOPK_GROUPS = 4
GROUP_SIZE = N_EXP // N_GROUPS
ROUTED_SCALE = 2.5
EPS = 1e-6
MOD_ROWS = 16
ROW_TILE = 256
MAX_WINDOW = 2048
GROUP = 128
SCATTER_BATCH = 16
RANK_CHUNK = 256
TOK_SPLIT = 64
FFN_TILE = 512
EXPERTS_PER_STEP = 4
SUB = 8
VMEM_LIMIT = 56 * 1024 * 1024
NEG_BIG = -1e30
LOG2_E = math.log2(math.e)

_NT = (((1,), (1,)), ((), ()))


def _params(*sem):
    return pltpu.CompilerParams(dimension_semantics=sem, vmem_limit_bytes=VMEM_LIMIT)


def _sigmoid(v):
    return 1.0 / (1.0 + jnp.exp(-v))


def _silu(v):
    return v * _sigmoid(v)


def _rms(v, gain):
    return v * lax.rsqrt(jnp.mean(v * v, axis=-1, keepdims=True) + EPS) * gain


def _dot(a, b):
    return jnp.dot(a, b, preferred_element_type=F32)


def _split2(v):
    hi = v.astype(BF16)
    lo = (v - hi.astype(F32)).astype(BF16)
    return hi, lo


def _split3(v):
    hi = v.astype(BF16)
    r = v - hi.astype(F32)
    mid = r.astype(BF16)
    lo = (r - mid.astype(F32)).astype(BF16)
    return hi, mid, lo


def _mod_kernel(c_ref, w_ref, b_ref, o_ref):
    s = _silu(c_ref[...])
    o_ref[0] = jnp.dot(s, w_ref[0], preferred_element_type=F32, precision=lax.Precision.HIGHEST) + b_ref[0]


def _modulation(cvec, w_mod, b_mod):
    depth, d, six_d = w_mod.shape
    tn = six_d // 4
    return pl.pallas_call(
        _mod_kernel,
        grid=(depth, six_d // tn),
        in_specs=[pl.BlockSpec((MOD_ROWS, d), lambda l, j: (0, 0)),
                  pl.BlockSpec((1, d, tn), lambda l, j: (l, 0, j)),
                  pl.BlockSpec((1, 1, tn), lambda l, j: (l, 0, j))],
        out_specs=pl.BlockSpec((1, MOD_ROWS, tn), lambda l, j: (l, 0, j)),
        out_shape=jax.ShapeDtypeStruct((depth, MOD_ROWS, six_d), F32),
        compiler_params=_params("arbitrary", "arbitrary"),
        name="modulation",
    )(cvec, w_mod, b_mod.reshape(depth, 1, six_d))


def _head_norm_rope(xb, gain, cos, sin):
    lane = lax.broadcasted_iota(jnp.int32, xb.shape, 1)
    low = lane < HEAD_DIM
    sq = xb * xb
    s_lo = jnp.sum(jnp.where(low, sq, 0.0), axis=-1, keepdims=True)
    s_hi = jnp.sum(jnp.where(low, 0.0, sq), axis=-1, keepdims=True)
    ms = jnp.where(low, s_lo, s_hi) * (1.0 / HEAD_DIM)
    y = xb * lax.rsqrt(ms + EPS) * gain
    ahead = pltpu.roll(y, LANES - HEAD_DIM // 2, 1)
    behind = pltpu.roll(y, HEAD_DIM // 2, 1)
    first_half = (lane % HEAD_DIM) < (HEAD_DIM // 2)
    return y * cos + jnp.where(first_half, ahead, behind) * sin


def _inproj_kernel(x_ref, mod_ref, g_ref, w_ref, qg_ref, kg_ref, cos_ref, sin_ref,
                   q_ref, k_ref, v_ref, z_ref, xbc_ref, dt_ref):
    h = _rms(x_ref[...], g_ref[...]) * (1.0 + mod_ref[0, 1:2, :]) + mod_ref[0, 0:1, :]
    hb = h.astype(BF16)
    cos = cos_ref[...]
    sin = sin_ref[...]
    c0 = 0
    for blk in range(ATT_W // LANES):
        qb = _dot(hb, w_ref[:, c0:c0 + LANES])
        qb = _head_norm_rope(qb, qg_ref[...], cos, sin) * (HEAD_DIM ** -0.5 * LOG2_E)
        q_ref[:, c0:c0 + LANES] = qb.astype(q_ref.dtype)
        c0 += LANES
    kb = _head_norm_rope(_dot(hb, w_ref[:, c0:c0 + LANES]), kg_ref[...], cos, sin)
    c0 += LANES
    vb = _dot(hb, w_ref[:, c0:c0 + LANES])
    c0 += LANES
    for g in range(KV_HEADS):
        k_ref[g] = kb[:, g * HEAD_DIM:(g + 1) * HEAD_DIM].astype(k_ref.dtype)
        v_ref[g] = vb[:, g * HEAD_DIM:(g + 1) * HEAD_DIM].astype(v_ref.dtype)
    z_ref[...] = _dot(hb, w_ref[:, c0:c0 + SSD_W])
    c0 += SSD_W
    xbc_ref[...] = _dot(hb, w_ref[:, c0:c0 + CONV_W])
    c0 += CONV_W
    dt_ref[...] = _dot(hb, w_ref[:, c0:c0 + LANES])


def _mod_row(start_row, n_lat, seq, n_batch):
    return jnp.where(start_row < n_lat, start_row // seq, n_batch)


def _inproj(xs, mod, g_pre, w_in, q_gain, k_gain, cos_t, sin_t, n_batch, seq, ctx_len):
    n, d = xs.shape
    n_lat = n_batch * seq
    tm = ROW_TILE
    lat_tiles = seq // tm
    ctx_tiles = ctx_len // tm

    def tab_idx(i):
        return (jnp.where(i * tm < n_lat, i % lat_tiles, lat_tiles + (i - n_lat // tm) % ctx_tiles), 0)

    row = lambda i: (i, 0)
    const = lambda i: (0, 0)
    return pl.pallas_call(
        _inproj_kernel,
        grid=(n // tm,),
        in_specs=[pl.BlockSpec((tm, d), row),
                  pl.BlockSpec((1, 6, d), lambda i: (_mod_row(i * tm, n_lat, seq, n_batch), 0, 0)),
                  pl.BlockSpec((1, d), const),
                  pl.BlockSpec((d, IN_PAD), const),
                  pl.BlockSpec((1, LANES), const),
                  pl.BlockSpec((1, LANES), const),
                  pl.BlockSpec((tm, LANES), tab_idx),
                  pl.BlockSpec((tm, LANES), tab_idx)],
        out_specs=[pl.BlockSpec((tm, ATT_W), row),
                   pl.BlockSpec((KV_HEADS, tm, HEAD_DIM), lambda i: (0, i, 0)),
                   pl.BlockSpec((KV_HEADS, tm, HEAD_DIM), lambda i: (0, i, 0)),
                   pl.BlockSpec((tm, SSD_W), row),
                   pl.BlockSpec((tm, CONV_W), row),
                   pl.BlockSpec((tm, LANES), row)],
        out_shape=[jax.ShapeDtypeStruct((n, ATT_W), BF16),
                   jax.ShapeDtypeStruct((KV_HEADS, n, HEAD_DIM), BF16),
                   jax.ShapeDtypeStruct((KV_HEADS, n, HEAD_DIM), BF16),
                   jax.ShapeDtypeStruct((n, SSD_W), F32),
                   jax.ShapeDtypeStruct((n, CONV_W), F32),
                   jax.ShapeDtypeStruct((n, LANES), F32)],
        compiler_params=_params("arbitrary"),
        name="inproj",
    )(xs, mod, g_pre, w_in, q_gain, k_gain, cos_t, sin_t)


def _attn_heads(q_ref, kc_ref, vc_ref, kl_ref, vl_ref, o_ref, latent):
    outs = []
    q = q_ref[...]

    def scores(h):
        g = h // Q_PER_KV
        qh = q[:, h * HEAD_DIM:(h + 1) * HEAD_DIM]
        sc = lax.dot_general(qh, kc_ref[g], _NT, preferred_element_type=F32)
        sl = lax.dot_general(qh, kl_ref[g], _NT, preferred_element_type=F32) if latent else None
        return sc, sl

    ahead = scores(0)
    for h in range(HEADS):
        g = h // Q_PER_KV
        sc, sl = ahead
        if h + 1 < HEADS:
            ahead = scores(h + 1)
        m = jnp.max(sc, axis=-1, keepdims=True)
        if latent:
            m = jnp.maximum(m, jnp.max(sl, axis=-1, keepdims=True))
            pw = jnp.exp2(sl - m)
        pc = jnp.exp2(sc - m)
        den = jnp.sum(pc, axis=-1, keepdims=True)
        acc = _dot(pc.astype(BF16), vc_ref[g])
        if latent:
            den = den + jnp.sum(pw, axis=-1, keepdims=True)
            acc = acc + _dot(pw.astype(BF16), vl_ref[g])
        outs.append(acc / den)
    o_ref[...] = jnp.concatenate(outs, axis=-1).astype(o_ref.dtype)


def _attn_kernel(n_lat_q, with_ctx, q_ref, kc_ref, vc_ref, kl_ref, vl_ref, o_ref):
    if not with_ctx:
        _attn_heads(q_ref, kc_ref, vc_ref, kl_ref, vl_ref, o_ref, True)
        return
    j = pl.program_id(1)

    @pl.when(j < n_lat_q)
    def _():
        _attn_heads(q_ref, kc_ref, vc_ref, kl_ref, vl_ref, o_ref, True)

    @pl.when(j >= n_lat_q)
    def _():
        _attn_heads(q_ref, kc_ref, vc_ref, kl_ref, vl_ref, o_ref, False)


def _attention(q, k, v, n_batch, seq, ctx_len, with_ctx):
    n = q.shape[0]
    n_lat = n_batch * seq
    tq = ROW_TILE
    n_lat_q = seq // tq
    n_ctx_q = ctx_len // tq if with_ctx else 0

    def q_idx(b, j):
        return (jnp.where(j < n_lat_q, b * n_lat_q + j, n_lat // tq + b * (ctx_len // tq) + (j - n_lat_q)), 0)

    ctx_idx = lambda b, j: (0, n_lat // ctx_len + b, 0)
    lat_idx = lambda b, j: (0, b, 0)
    return pl.pallas_call(
        functools.partial(_attn_kernel, n_lat_q, with_ctx),
        grid=(n_batch, n_lat_q + n_ctx_q),
        in_specs=[pl.BlockSpec((tq, ATT_W), q_idx),
                  pl.BlockSpec((KV_HEADS, ctx_len, HEAD_DIM), ctx_idx),
                  pl.BlockSpec((KV_HEADS, ctx_len, HEAD_DIM), ctx_idx),
                  pl.BlockSpec((KV_HEADS, seq, HEAD_DIM), lat_idx),
                  pl.BlockSpec((KV_HEADS, seq, HEAD_DIM), lat_idx)],
        out_specs=pl.BlockSpec((tq, ATT_W), q_idx),
        out_shape=jax.ShapeDtypeStruct((n if with_ctx else n_lat, ATT_W), BF16),
        compiler_params=_params("arbitrary", "arbitrary"),
        name="attention",
    )(q, k, v, k, v)


def _conv_chunk(ncc, nlc, u_ref, p_ref, nx_ref, w_ref, b_ref):
    j = pl.program_id(1)
    pos = jnp.where(j < ncc, j, j - ncc)
    last_pos = jnp.where(j < ncc, ncc - 1, nlc - 1)
    u = u_ref[...]
    r = lax.broadcasted_iota(jnp.int32, u.shape, 0)
    prev_row = jnp.where(pos == 0, 0.0, p_ref[SUB - 1:SUB, :])
    next_row = jnp.where(pos == last_pos, 0.0, nx_ref[0:1, :])
    up = jnp.where(r == 0, prev_row, pltpu.roll(u, 1, 0))
    dn = jnp.where(r == CHUNK - 1, next_row, pltpu.roll(u, CHUNK - 1, 0))
    return _silu(w_ref[0:1, :] * up + w_ref[1:2, :] * u + w_ref[2:3, :] * dn + b_ref[...])


def _ssd_kernel(direction, ncc, nlc, *refs):
    if direction == 0:
        (u_ref, p_ref, nx_ref, cw_ref, cb_ref, dt_ref, dtb_ref, a_ref, ex_ref, y_ref, xc_ref, st_ref) = refs
        xc = _conv_chunk(ncc, nlc, u_ref, p_ref, nx_ref, cw_ref, cb_ref)
        xc_ref[...] = xc
    else:
        (xc_in_ref, dt_ref, dtb_ref, a_ref, ex_ref, yf_ref, z_ref, dsk_ref, gn_ref, y_ref, st_ref) = refs
        xc = xc_in_ref[...]

    @pl.when(pl.program_id(1) == 0)
    def _():
        st_ref[...] = jnp.zeros_like(st_ref)

    xs = xc[:, 0:SSD_W]
    b_all = xc[:, SSD_W:SSD_W + SSD_G * SSD_N]
    c_all = xc[:, SSD_W + SSD_G * SSD_N:]
    gw = SSD_W // SSD_G
    hpg = SSD_HEADS // SSD_G
    st = st_ref[...]
    st_b = st.astype(BF16)
    bgs = [b_all[:, g * SSD_N:(g + 1) * SSD_N] for g in range(SSD_G)]
    cgs = [c_all[:, g * SSD_N:(g + 1) * SSD_N].astype(BF16) for g in range(SSD_G)]
    grams = [lax.dot_general(cgs[g], bgs[g].astype(BF16), _NT, preferred_element_type=F32) for g in range(SSD_G)]
    carried = [_dot(cgs[g], st_b[:, g * gw:(g + 1) * gw]) for g in range(SSD_G)]
    bts = [bgs[g].T.astype(BF16) for g in range(SSD_G)]
    pre = dt_ref[...] + dtb_ref[...]
    dtv = jnp.maximum(pre, 0.0) + jnp.log1p(jnp.exp(-jnp.abs(pre)))
    la = dtv * a_ref[...]
    row = lax.broadcasted_iota(jnp.int32, (CHUNK, CHUNK), 0)
    col = lax.broadcasted_iota(jnp.int32, (CHUNK, CHUNK), 1)
    tri = (col <= row) if direction == 0 else (col >= row)
    tri_b = jnp.where(tri, 1.0, 0.0).astype(BF16)
    cs = sum(_dot(tri_b, part) for part in _split3(la))
    ex = ex_ref[...]

    def expand(val):
        return sum(_dot(part, ex) for part in _split2(val))

    edge = CHUNK - 1 if direction == 0 else 0
    tot = cs[edge:edge + 1, :]
    dt_e = expand(dtv)
    da_e = expand(jnp.exp(cs))
    db_e = expand(jnp.exp(tot - cs))
    dtot_e = da_e[edge:edge + 1, :]
    xd = xs * dt_e
    xd_end = (xd * db_e).astype(BF16)
    cs_t = cs.T
    ys = []
    new_st = []
    for g in range(SSD_G):
        y_off = carried[g] * da_e[:, g * gw:(g + 1) * gw]
        new_st.append(st[:, g * gw:(g + 1) * gw] * dtot_e[:, g * gw:(g + 1) * gw]
                      + _dot(bts[g], xd_end[:, g * gw:(g + 1) * gw]))
        for hh in range(hpg):
            h = g * hpg + hh
            c = direction * SSD_HEADS + h
            diff = cs[:, c:c + 1] - cs_t[c:c + 1, :]
            decay = jnp.exp(jnp.where(tri, diff, NEG_BIG))
            m = (grams[g] * decay).astype(BF16)
            y_d = _dot(m, xd[:, h * SSD_P:(h + 1) * SSD_P].astype(BF16))
            ys.append(y_d + y_off[:, hh * SSD_P:(hh + 1) * SSD_P])
    st_ref[...] = jnp.concatenate(new_st, axis=-1)
    y = jnp.concatenate(ys, axis=-1)
    if direction == 0:
        y_ref[...] = y
    else:
        z = z_ref[...]
        gated = (yf_ref[...] + y + xs * dsk_ref[...]) * _silu(z)
        y_ref[...] = _rms(gated, gn_ref[...]).astype(y_ref.dtype)


def _ssd(direction, xin, dt, dtb, a_neg, ex, n_batch, seq, ctx_len, extra):
    n, cw = xin.shape
    n_lat = n_batch * seq
    ncc = ctx_len // CHUNK
    nlc = seq // CHUNK
    per = CHUNK // SUB

    def rows(b, j):
        if direction == 0:
            return jnp.where(j < ncc, n_lat // CHUNK + b * ncc + j, b * nlc + (j - ncc))
        return jnp.where(j < ncc, n_lat // CHUNK + b * ncc + (ncc - 1 - j), b * nlc + (nlc - 1 - (j - ncc)))

    const = lambda b, j: (0, 0)
    chunk = lambda width: pl.BlockSpec((CHUNK, width), lambda b, j: (rows(b, j), 0))
    scan_specs = [chunk(LANES), pl.BlockSpec((1, LANES), const), pl.BlockSpec((1, LANES), const),
                  pl.BlockSpec((LANES, SSD_W), const)]
    if direction == 0:
        conv_w, conv_b = extra
        in_specs = [chunk(cw),
                    pl.BlockSpec((SUB, cw), lambda b, j: (jnp.maximum(rows(b, j) * per - 1, 0), 0)),
                    pl.BlockSpec((SUB, cw), lambda b, j: (jnp.minimum((rows(b, j) + 1) * per, n // SUB - 1), 0)),
                    pl.BlockSpec((3, cw), const), pl.BlockSpec((1, cw), const)] + scan_specs
        args = [xin, xin, xin, conv_w, conv_b, dt, dtb, a_neg, ex]
        out_specs = [chunk(SSD_W), chunk(cw)]
        out_shape = [jax.ShapeDtypeStruct((n, SSD_W), F32), jax.ShapeDtypeStruct((n, cw), F32)]
    else:
        yf, z, dskip, gnorm = extra
        in_specs = [chunk(cw)] + scan_specs + [chunk(SSD_W), chunk(SSD_W), pl.BlockSpec((1, SSD_W), const),
                                               pl.BlockSpec((1, SSD_W), const)]
        args = [xin, dt, dtb, a_neg, ex, yf, z, dskip, gnorm]
        out_specs = chunk(SSD_W)
        out_shape = jax.ShapeDtypeStruct((n, SSD_W), BF16)
    return pl.pallas_call(
        functools.partial(_ssd_kernel, direction, ncc, nlc),
        grid=(n_batch, ncc + nlc),
        in_specs=in_specs,
        out_specs=out_specs,
        out_shape=out_shape,
        scratch_shapes=[pltpu.VMEM((SSD_N, SSD_W), F32)],
        compiler_params=_params("arbitrary", "arbitrary"),
        name="ssd_fwd" if direction == 0 else "ssd_bwd",
    )(*args)


def _route(logits, bias):
    tokens = logits.shape[1]
    scores = _sigmoid(logits)
    choice = scores + bias
    sub = lax.broadcasted_iota(jnp.int32, (GROUP_SIZE, tokens), 0).astype(F32)
    neg = -jnp.inf

    def pick_first_max(v, idx, sentinel):
        m = jnp.max(v, axis=0, keepdims=True)
        first = jnp.min(jnp.where(v == m, idx, sentinel), axis=0, keepdims=True)
        return m, idx == first

    group_scores = []
    for g in range(N_GROUPS):
        cg = choice[g * GROUP_SIZE:(g + 1) * GROUP_SIZE, :]
        m1, hit = pick_first_max(cg, sub, GROUP_SIZE)
        m2 = jnp.max(jnp.where(hit, neg, cg), axis=0, keepdims=True)
        group_scores.append(m1 + m2)
    v = jnp.concatenate(group_scores, axis=0)
    gsel = jnp.zeros_like(v)
    for _ in range(TOPK_GROUPS):
        _, hit = pick_first_max(v, sub, N_GROUPS)
        gsel = jnp.where(hit, 1.0, gsel)
        v = jnp.where(hit, neg, v)
    ok = jnp.concatenate([jnp.broadcast_to(gsel[g:g + 1, :], (GROUP_SIZE, tokens)) for g in range(N_GROUPS)],
                         axis=0) > 0.5
    v = jnp.where(ok, choice, neg)
    eidx = lax.broadcasted_iota(jnp.int32, (N_EXP, tokens), 0).astype(F32)
    ids = []
    gates = []
    for _ in range(TOP_K):
        _, hit = pick_first_max(v, eidx, N_EXP)
        ids.append(jnp.sum(jnp.where(hit, eidx, 0.0), axis=0, keepdims=True))
        gates.append(jnp.sum(jnp.where(hit, scores, 0.0), axis=0, keepdims=True))
        v = jnp.where(hit, neg, v)
    ids = jnp.concatenate(ids, axis=0)
    gates = jnp.concatenate(gates, axis=0)
    gates = gates / jnp.sum(gates, axis=0, keepdims=True) * ROUTED_SCALE
    return ids.astype(jnp.int32), gates


def _store_slabs(slab_ref, rows):
    for kc in range(SUB):
        slab_ref[pl.ds(kc, rows.shape[0], stride=SUB), :] = rows[:, kc * LANES:(kc + 1) * LANES]


def _load_slabs(slab_ref, n):
    return jnp.concatenate([slab_ref[pl.ds(kc, n, stride=SUB), :] for kc in range(SUB)], axis=1)


def _outproj_kernel(att_ref, ssd_ref, x_ref, wo_ref, mod_ref, gpost_ref, gpre_ref, wrh_ref, wrl_ref, rb_ref,
                    x1_ref, h2_ref, ids_ref, gates_ref):
    m = _dot(att_ref[...], wo_ref[0:ATT_W, :]) + _dot(ssd_ref[...], wo_ref[ATT_W:, :])
    x1 = x_ref[...] + mod_ref[0, 2:3, :] * _rms(m, gpost_ref[...])
    x1_ref[...] = x1
    h2 = _rms(x1, gpre_ref[...]) * (1.0 + mod_ref[0, 4:5, :]) + mod_ref[0, 3:4, :]
    _store_slabs(h2_ref, h2)
    h_hi, h_lo = _split2(h2)
    nt = lambda a, b: lax.dot_general(a, b, _NT, preferred_element_type=F32)
    logits = nt(wrh_ref[...], h_hi) + (nt(wrh_ref[...], h_lo) + nt(wrl_ref[...], h_hi))
    ids_ref[...], gates_ref[...] = _route(logits, rb_ref[...])


def _outproj(att, ssd, xs, w_out, mod, g_post, g_pre, wr_t, r_bias, n_rows, n_batch, seq):
    d = xs.shape[1]
    n_lat = n_batch * seq
    tm = FFN_TILE
    wr_hi = wr_t.astype(BF16)
    wr_lo = (wr_t - wr_hi.astype(F32)).astype(BF16)
    row = lambda i: (i, 0)
    const = lambda i: (0, 0)
    return pl.pallas_call(
        _outproj_kernel,
        grid=(n_rows // tm,),
        in_specs=[pl.BlockSpec((tm, ATT_W), row),
                  pl.BlockSpec((tm, SSD_W), row),
                  pl.BlockSpec((tm, d), row),
                  pl.BlockSpec((ATT_W + SSD_W, d), const),
                  pl.BlockSpec((1, 6, d), lambda i: (_mod_row(i * tm, n_lat, seq, n_batch), 0, 0)),
                  pl.BlockSpec((1, d), const),
                  pl.BlockSpec((1, d), const),
                  pl.BlockSpec((N_EXP, d), const),
                  pl.BlockSpec((N_EXP, d), const),
                  pl.BlockSpec((N_EXP, 1), const)],
        out_specs=[pl.BlockSpec((tm, d), row),
                   pl.BlockSpec((tm * SUB, LANES), row),
                   pl.BlockSpec((TOP_K, tm), lambda i: (0, i)),
                   pl.BlockSpec((TOP_K, tm), lambda i: (0, i))],
        out_shape=[jax.ShapeDtypeStruct((n_rows, d), F32),
                   jax.ShapeDtypeStruct((n_rows * SUB, LANES), F32),
                   jax.ShapeDtypeStruct((TOP_K, n_rows), jnp.int32),
                   jax.ShapeDtypeStruct((TOP_K, n_rows), F32)],
        compiler_params=_params("arbitrary"),
        name="outproj_router",
    )(att, ssd, xs, w_out, mod, g_post, g_pre, wr_hi, wr_lo, r_bias)


def _tables_kernel(win, n_groups, ids_ref, gates_ref, list_ref, gl_ref, first_ref, ngr_ref):
    ids = ids_ref[...]
    gts = gates_ref[...]
    eidx = lax.broadcasted_iota(jnp.int32, (N_EXP, win), 0)
    hits = [ids[k:k + 1, :] == eidx for k in range(TOP_K)]
    sel = sum(jnp.where(h, 1.0, 0.0) for h in hits)
    cnt = jnp.sum(sel, axis=1, keepdims=True)
    ngr = jnp.floor((cnt + (GROUP - 1)) * (1.0 / GROUP))
    er = lax.broadcasted_iota(jnp.int32, (N_EXP, N_EXP), 0)
    ec = lax.broadcasted_iota(jnp.int32, (N_EXP, N_EXP), 1)
    below = jnp.where(ec < er, 1.0, 0.0).astype(BF16)
    ngr_b = jnp.broadcast_to(ngr, (N_EXP, LANES))
    first = _dot(below, ngr_b.astype(BF16)) + 1.0
    cr = lax.broadcasted_iota(jnp.int32, (RANK_CHUNK, RANK_CHUNK), 0)
    cc = lax.broadcasted_iota(jnp.int32, (RANK_CHUNK, RANK_CHUNK), 1)
    before = jnp.where(cr < cc, 1.0, 0.0).astype(BF16)
    carry = jnp.zeros((N_EXP, 1), F32)
    ranks = []
    for c0 in range(0, win, RANK_CHUNK):
        s = sel[:, c0:c0 + RANK_CHUNK]
        ranks.append(_dot(s.astype(BF16), before) + carry)
        carry = carry + jnp.sum(s, axis=1, keepdims=True)
    slot_of = first[:, 0:1] * GROUP + jnp.concatenate(ranks, axis=1)
    slot = jnp.concatenate([jnp.sum(jnp.where(h, slot_of, 0.0), axis=0, keepdims=True) for h in hits], axis=0)
    hi = jnp.floor(slot * (1.0 / GROUP))
    lo = slot - hi * GROUP
    g1 = gts.astype(BF16).astype(F32)
    rest = gts - g1
    g2 = rest.astype(BF16).astype(F32)
    g3 = rest - g2
    stack = jnp.concatenate([lo, g1, g2, g3, jnp.zeros((LANES - 4 * TOP_K, win), F32)], axis=0)
    cols = stack.T
    tok1 = lax.broadcasted_iota(jnp.int32, (win, LANES), 0) + 1
    tok_a = (tok1 // TOK_SPLIT).astype(F32)
    tok_b = (tok1 % TOK_SPLIT).astype(F32)
    lane = lax.broadcasted_iota(jnp.int32, (win, LANES), 1).astype(F32)
    gidx = lax.broadcasted_iota(jnp.int32, (n_groups, win), 0).astype(F32)
    acc = jnp.zeros((n_groups, 5 * LANES), F32)
    for k in range(TOP_K):
        onehot = cols[:, k:k + 1] == lane
        vals = (tok_a, tok_b, cols[:, TOP_K + k:TOP_K + k + 1], cols[:, 2 * TOP_K + k:2 * TOP_K + k + 1],
                cols[:, 3 * TOP_K + k:3 * TOP_K + k + 1])
        rhs = jnp.concatenate([jnp.where(onehot, v, 0.0) for v in vals], axis=1).astype(BF16)
        lhs = jnp.where(hi[k:k + 1, :] == gidx, 1.0, 0.0).astype(BF16)
        acc = acc + _dot(lhs, rhs)
    tok1_tab = acc[:, 0:LANES] * TOK_SPLIT + acc[:, LANES:2 * LANES]
    gate_tab = acc[:, 2 * LANES:3 * LANES] + acc[:, 3 * LANES:4 * LANES] + acc[:, 4 * LANES:5 * LANES]
    spare = lax.broadcasted_iota(jnp.int32, (n_groups, LANES), 1).astype(F32) + win
    row_of = jnp.where(tok1_tab > 0.5, tok1_tab - 1.0, spare) * SUB
    list_ref[0] = row_of.astype(jnp.int32)
    gl_ref[0] = gate_tab
    first_ref[0] = first.astype(jnp.int32)
    ngr_ref[0] = ngr_b.astype(jnp.int32)


def _tables(ids_t, gates_t, win):
    n = ids_t.shape[1]
    n_win = n // win
    n_groups = win * TOP_K // GROUP + N_EXP + SUB
    blk = lambda w: (0, w)
    out = lambda w: (w, 0, 0)
    return pl.pallas_call(
        functools.partial(_tables_kernel, win, n_groups),
        grid=(n_win,),
        in_specs=[pl.BlockSpec((TOP_K, win), blk), pl.BlockSpec((TOP_K, win), blk)],
        out_specs=[pl.BlockSpec((1, n_groups, LANES), out), pl.BlockSpec((1, n_groups, LANES), out),
                   pl.BlockSpec((1, N_EXP, LANES), out), pl.BlockSpec((1, N_EXP, LANES), out)],
        out_shape=[jax.ShapeDtypeStruct((n_win, n_groups, LANES), jnp.int32),
                   jax.ShapeDtypeStruct((n_win, n_groups, LANES), F32),
                   jax.ShapeDtypeStruct((n_win, N_EXP, LANES), jnp.int32),
                   jax.ShapeDtypeStruct((n_win, N_EXP, LANES), jnp.int32)],
        compiler_params=_params("arbitrary"),
        name="dispatch_tables",
    )(ids_t, gates_t)


def _moe_kernel(win, first_ref, ngr_ref, lst_ref, gl_ref, src_ref, wg_ref, wu_ref, wd_ref, o_ref,
                buf0_ref, buf1_ref, ybuf0_ref, ybuf1_ref):
    w = pl.program_id(0)
    step = pl.program_id(1)
    acc_ref = o_ref.at[0]
    last_row = (win - 1) * SUB
    bufs = (buf0_ref, buf1_ref)
    ybufs = (ybuf0_ref, ybuf1_ref)

    def gather(g, buf_ref):
        for r in range(GROUP):
            row = pl.multiple_of(jnp.minimum(lst_ref[0, g, r], last_row), SUB)
            buf_ref[pl.ds(r * SUB, SUB), :] = src_ref[pl.ds(row, SUB), :]

    def scatter(g, ybuf_ref):
        for r0 in range(0, GROUP, SCATTER_BATCH):
            rows = [pl.multiple_of(lst_ref[0, g, r0 + u], SUB) for u in range(SCATTER_BATCH)]
            vals = [acc_ref[pl.ds(rows[u], SUB), :] + ybuf_ref[pl.ds((r0 + u) * SUB, SUB), :]
                    for u in range(SCATTER_BATCH)]
            for u in range(SCATTER_BATCH):
                acc_ref[pl.ds(rows[u], SUB), :] = vals[u]

    def expert(g, k, buf_ref, ybuf_ref):
        x = _load_slabs(buf_ref, GROUP).astype(BF16)
        a = _silu(_dot(x, wg_ref[k])) * _dot(x, wu_ref[k])
        gate_row = jnp.broadcast_to(gl_ref[0, pl.ds(g, 1), :], (GROUP, LANES))
        diag = (lax.broadcasted_iota(jnp.int32, (GROUP, LANES), 0)
                == lax.broadcasted_iota(jnp.int32, (GROUP, LANES), 1))
        gate_col = jnp.sum(jnp.where(diag, gate_row, 0.0), axis=1, keepdims=True)
        _store_slabs(ybuf_ref, _dot((a * gate_col).astype(BF16), wd_ref[k]))

    def stage(g, k, p):
        gather(g + 1, bufs[1 - p])
        expert(g, k, bufs[p], ybufs[p])
        scatter(g - 1, ybufs[1 - p])

    @pl.when(step == 0)
    def _():
        o_ref[...] = jnp.zeros_like(o_ref)
        ybuf0_ref[...] = jnp.zeros_like(ybuf0_ref)
        gather(1, bufs[1])

    g_end = 0
    for k in range(EXPERTS_PER_STEP):
        e = w * N_EXP + step * EXPERTS_PER_STEP + k
        g0 = first_ref[e]
        n = ngr_ref[e]
        g_end = g0 + n

        def body(i, carry, g0=g0, k=k):
            g = g0 + i
            for p in range(2):
                pl.when(g % 2 == p)(functools.partial(stage, g, k, p))
            return carry

        lax.fori_loop(0, n, body, 0)

    @pl.when(step == pl.num_programs(1) - 1)
    def _():
        for p in range(2):
            pl.when((g_end - 1) % 2 == p)(functools.partial(scatter, g_end - 1, ybufs[p]))


def _moe(h2_slabs, ids_t, gates_t, w_gate, w_up, w_down, layer, win):
    n_rows = ids_t.shape[1]
    n_win = n_rows // win
    d, fe = w_gate.shape[2:]
    lists, gate_tab, first, ngr = _tables(ids_t, gates_t, win)
    n_groups = lists.shape[1]
    slab = lambda w, s, first, ngr: (w, 0)
    tab = lambda w, s, first, ngr: (w, 0, 0)
    wspec = lambda w, s, first, ngr: (layer, s, 0, 0)
    eps = EXPERTS_PER_STEP
    return pl.pallas_call(
        functools.partial(_moe_kernel, win),
        grid_spec=pltpu.PrefetchScalarGridSpec(
            num_scalar_prefetch=2,
            grid=(n_win, N_EXP // eps),
            in_specs=[pl.BlockSpec((1, n_groups, LANES), tab, memory_space=pltpu.SMEM),
                      pl.BlockSpec((1, n_groups, LANES), tab),
                      pl.BlockSpec((win * SUB, LANES), slab),
                      pl.BlockSpec((None, eps, d, fe), wspec),
                      pl.BlockSpec((None, eps, d, fe), wspec),
                      pl.BlockSpec((None, eps, fe, d), wspec)],
            out_specs=pl.BlockSpec((1, (win + GROUP) * SUB, LANES), tab),
            scratch_shapes=[pltpu.VMEM((GROUP * SUB, LANES), F32)] * 4),
        out_shape=jax.ShapeDtypeStruct((n_win, (win + GROUP) * SUB, LANES), F32),
        compiler_params=_params("arbitrary", "arbitrary"),
        name="moe_routed",
    )(first[:, :, 0].reshape(-1), ngr[:, :, 0].reshape(-1), lists, gate_tab, h2_slabs, w_gate, w_up, w_down)


def _ffn_out_kernel(r_ref, h_ref, x1_ref, mod_ref, gpost_ref, sg_ref, su_ref, sd_ref, o_ref):
    tm = x1_ref.shape[0]
    h = _load_slabs(h_ref, tm).astype(BF16)
    a = _silu(_dot(h, sg_ref[...])) * _dot(h, su_ref[...])
    y = _load_slabs(r_ref.at[0], tm) + _dot(a.astype(BF16), sd_ref[...])
    o_ref[...] = x1_ref[...] + mod_ref[0, 5:6, :] * _rms(y, gpost_ref[...])


def _ffn_out(routed, h2_slabs, x1, mod, g_post, ws_gate, ws_up, ws_down, n_batch, seq, win):
    n_rows, d = x1.shape
    n_lat = n_batch * seq
    tm = FFN_TILE
    per_win = win // tm
    fs = ws_gate.shape[-1]
    row = lambda i: (i, 0)
    const = lambda i: (0, 0)
    return pl.pallas_call(
        _ffn_out_kernel,
        grid=(n_rows // tm,),
        in_specs=[pl.BlockSpec((1, tm * SUB, LANES), lambda i: (i // per_win, i % per_win, 0)),
                  pl.BlockSpec((tm * SUB, LANES), row),
                  pl.BlockSpec((tm, d), row),
                  pl.BlockSpec((1, 6, d), lambda i: (_mod_row(i * tm, n_lat, seq, n_batch), 0, 0)),
                  pl.BlockSpec((1, d), const),
                  pl.BlockSpec((d, fs), const),
                  pl.BlockSpec((d, fs), const),
                  pl.BlockSpec((fs, d), const)],
        out_specs=pl.BlockSpec((tm, d), row),
        out_shape=jax.ShapeDtypeStruct((n_rows, d), F32),
        compiler_params=_params("arbitrary"),
        name="ffn_out",
    )(routed, h2_slabs, x1, mod, g_post, ws_gate, ws_up, ws_down)


def _rope_tables(seq, ctx_len):
    rows = seq // GRID_W
    row = jnp.repeat(jnp.arange(rows), GRID_W).astype(F32)
    col = jnp.tile(jnp.arange(GRID_W), rows).astype(F32)
    n_freq = HEAD_DIM // 4
    inv_freq = ROPE_THETA ** (-jnp.arange(n_freq, dtype=F32) / n_freq)
    ang = jnp.concatenate([row[:, None] * inv_freq, col[:, None] * inv_freq], axis=-1)
    reps = LANES // (HEAD_DIM // 2)
    cos = jnp.tile(jnp.cos(ang), (1, reps))
    sign = jnp.where((jnp.arange(LANES) % HEAD_DIM) < HEAD_DIM // 2, -1.0, 1.0).astype(F32)
    sin = jnp.tile(jnp.sin(ang), (1, reps)) * sign
    cos = jnp.concatenate([cos, jnp.ones((ctx_len, LANES), F32)], axis=0)
    sin = jnp.concatenate([sin, jnp.zeros((ctx_len, LANES), F32)], axis=0)
    return cos, sin


def _pad_lanes(v):
    return jnp.pad(v.reshape(1, -1), ((0, 0), (0, LANES - v.size)))


def kernel(x, c, ctx, c_ctx, w_mod, b_mod, g_pre_mix, g_post_mix, g_pre_ffn, g_post_ffn, w_in, q_norm, k_norm,
           conv_w, conv_b, dt_bias, a_log, d_skip, ssd_norm, w_out, router_w, router_bias, w_gate, w_up, w_down,
           ws_gate, ws_up, ws_down):
    n_batch, seq, d = x.shape
    ctx_len = ctx.shape[1]
    depth = w_mod.shape[0]
    n_lat = n_batch * seq
    assert n_batch < MOD_ROWS and seq % ROW_TILE == 0 and ctx_len % ROW_TILE == 0 and seq % GRID_W == 0

    xs = jnp.concatenate([x.reshape(n_lat, d), ctx.reshape(n_batch * ctx_len, d)], axis=0)
    cvec = jnp.zeros((MOD_ROWS, d), F32).at[:n_batch].set(c).at[n_batch].set(c_ctx)
    mod_all = _modulation(cvec, w_mod, b_mod)
    cos_t, sin_t = _rope_tables(seq, ctx_len)
    head_lane = jnp.arange(LANES)[:, None]
    chan_head = jnp.arange(SSD_W)[None, :] // SSD_P
    win = math.gcd(MAX_WINDOW, math.gcd(n_lat, n_batch * ctx_len))
    w_gate_b, w_up_b, w_down_b = w_gate.astype(BF16), w_up.astype(BF16), w_down.astype(BF16)

    for i in range(depth):
        last = i == depth - 1
        mod = mod_all[i].reshape(MOD_ROWS, 6, d)
        w_in_b = jnp.pad(w_in[i].astype(BF16), ((0, 0), (0, IN_PAD - IN_W)))
        q, k, v, z, xbc, dt = _inproj(xs, mod, g_pre_mix[i].reshape(1, d), w_in_b,
                                      jnp.tile(q_norm[i], LANES // HEAD_DIM).reshape(1, LANES),
                                      jnp.tile(k_norm[i], LANES // HEAD_DIM).reshape(1, LANES),
                                      cos_t, sin_t, n_batch, seq, ctx_len)
        att = _attention(q, k, v, n_batch, seq, ctx_len, with_ctx=not last)
        dtb = _pad_lanes(dt_bias[i])
        a_neg = _pad_lanes(-jnp.exp(a_log[i]))
        y_f, xc = _ssd(0, xbc, dt, dtb, a_neg, (head_lane == chan_head).astype(BF16), n_batch, seq, ctx_len,
                       (conv_w[i], conv_b[i].reshape(1, CONV_W)))
        extra = (y_f, z, jnp.repeat(d_skip[i], SSD_P).reshape(1, SSD_W), ssd_norm[i].reshape(1, SSD_W))
        ssd = _ssd(1, xc, dt, dtb, a_neg, (head_lane == chan_head + SSD_HEADS).astype(BF16),
                   n_batch, seq, ctx_len, extra)
        n_rows = n_lat if last else xs.shape[0]
        x1, h2, ids_t, gates_t = _outproj(att, ssd, xs, w_out[i].astype(BF16), mod, g_post_mix[i].reshape(1, d),
                                          g_pre_ffn[i].reshape(1, d), router_w[i].T,
                                          router_bias[i].reshape(N_EXP, 1), n_rows, n_batch, seq)
        routed = _moe(h2, ids_t, gates_t, w_gate_b, w_up_b, w_down_b, i, win)
        xs = _ffn_out(routed, h2, x1, mod, g_post_ffn[i].reshape(1, d), ws_gate[i].astype(BF16),
                      ws_up[i].astype(BF16), ws_down[i].astype(BF16), n_batch, seq, win)
    return xs.reshape(n_batch, seq, d)
```

```python
import functools
import math

import jax
import jax.numpy as jnp
from jax import lax
from jax.experimental import pallas as pl
from jax.experimental.pallas import tpu as pltpu

F32 = jnp.float32
BF16 = jnp.bfloat16

GRID_W = 64
HEADS = 8
KV_HEADS = 2
HEAD_DIM = 64
Q_PER_KV = HEADS // KV_HEADS
ATT_W = HEADS * HEAD_DIM
ROPE_THETA = 10000.0
SSD_HEADS = 8
SSD_P = 64
SSD_W = SSD_HEADS * SSD_P
SSD_G = 2
SSD_N = 128
CHUNK = 128
SSD_CHUNKS = 2
CONV_W = SSD_W + 2 * SSD_G * SSD_N
IN_W = ATT_W + 2 * KV_HEADS * HEAD_DIM + SSD_W + CONV_W + 2 * SSD_HEADS
LANES = 128
IN_PAD = IN_W - 2 * SSD_HEADS + LANES
N_EXP = 64
TOP_K = 8
N_GROUPS = 8
TOPK_GROUPS = 4
GROUP_SIZE = N_EXP // N_GROUPS
ROUTED_SCALE = 2.5
EPS = 1e-6
MOD_ROWS = 16
ROW_TILE = 256
MAX_WINDOW = 2048
GROUP = 128
STAGE_LISTS = 1
STAGE_ROWS = GROUP * STAGE_LISTS
SCATTER_BATCH = 16
RANK_CHUNK = 256
TOK_SPLIT = 64
FFN_TILE = 512
EXPERTS_PER_STEP = 4
SUB = 8
VMEM_LIMIT = 56 * 1024 * 1024
NEG_BIG = -1e30
LOG2_E = math.log2(math.e)

_NT = (((1,), (1,)), ((), ()))


def _params(*sem):
    return pltpu.CompilerParams(dimension_semantics=sem, vmem_limit_bytes=VMEM_LIMIT)


def _sigmoid(v):
    return 1.0 / (1.0 + jnp.exp(-v))


def _silu(v):
    return v * _sigmoid(v)


def _rms(v, gain):
    return v * lax.rsqrt(jnp.mean(v * v, axis=-1, keepdims=True) + EPS) * gain


def _dot(a, b):
    return jnp.dot(a, b, preferred_element_type=F32)


def _split2(v):
    hi = v.astype(BF16)
    lo = (v - hi.astype(F32)).astype(BF16)
    return hi, lo


def _split3(v):
    hi = v.astype(BF16)
    r = v - hi.astype(F32)
    mid = r.astype(BF16)
    lo = (r - mid.astype(F32)).astype(BF16)
    return hi, mid, lo


def _mod_kernel(c_ref, w_ref, b_ref, o_ref):
    s = _silu(c_ref[...])
    o_ref[0] = jnp.dot(s, w_ref[0], preferred_element_type=F32, precision=lax.Precision.HIGHEST) + b_ref[0]


def _modulation(cvec, w_mod, b_mod):
    depth, d, six_d = w_mod.shape
    tn = six_d // 4
    return pl.pallas_call(
        _mod_kernel,
        grid=(depth, six_d // tn),
        in_specs=[pl.BlockSpec((MOD_ROWS, d), lambda l, j: (0, 0)),
                  pl.BlockSpec((1, d, tn), lambda l, j: (l, 0, j)),
                  pl.BlockSpec((1, 1, tn), lambda l, j: (l, 0, j))],
        out_specs=pl.BlockSpec((1, MOD_ROWS, tn), lambda l, j: (l, 0, j)),
        out_shape=jax.ShapeDtypeStruct((depth, MOD_ROWS, six_d), F32),
        compiler_params=_params("arbitrary", "arbitrary"),
        name="modulation",
    )(cvec, w_mod, b_mod.reshape(depth, 1, six_d))


def _head_norm_rope(xb, gain, cos, sin):
    lane = lax.broadcasted_iota(jnp.int32, xb.shape, 1)
    low = lane < HEAD_DIM
    sq = xb * xb
    s_lo = jnp.sum(jnp.where(low, sq, 0.0), axis=-1, keepdims=True)
    s_hi = jnp.sum(jnp.where(low, 0.0, sq), axis=-1, keepdims=True)
    ms = jnp.where(low, s_lo, s_hi) * (1.0 / HEAD_DIM)
    y = xb * lax.rsqrt(ms + EPS) * gain
    ahead = pltpu.roll(y, LANES - HEAD_DIM // 2, 1)
    behind = pltpu.roll(y, HEAD_DIM // 2, 1)
    first_half = (lane % HEAD_DIM) < (HEAD_DIM // 2)
    return y * cos + jnp.where(first_half, ahead, behind) * sin


def _inproj_kernel(x_ref, mod_ref, g_ref, w_ref, qg_ref, kg_ref, cos_ref, sin_ref,
                   q_ref, k_ref, v_ref, z_ref, xbc_ref, dt_ref):
    h = _rms(x_ref[...], g_ref[...]) * (1.0 + mod_ref[0, 1:2, :]) + mod_ref[0, 0:1, :]
    hb = h.astype(BF16)
    cos = cos_ref[...]
    sin = sin_ref[...]
    c0 = 0
    for blk in range(ATT_W // LANES):
        qb = _dot(hb, w_ref[:, c0:c0 + LANES])
        qb = _head_norm_rope(qb, qg_ref[...], cos, sin) * (HEAD_DIM ** -0.5 * LOG2_E)
        q_ref[:, c0:c0 + LANES] = qb.astype(q_ref.dtype)
        c0 += LANES
    kb = _head_norm_rope(_dot(hb, w_ref[:, c0:c0 + LANES]), kg_ref[...], cos, sin)
    c0 += LANES
    vb = _dot(hb, w_ref[:, c0:c0 + LANES])
    c0 += LANES
    for g in range(KV_HEADS):
        k_ref[g] = kb[:, g * HEAD_DIM:(g + 1) * HEAD_DIM].astype(k_ref.dtype)
        v_ref[g] = vb[:, g * HEAD_DIM:(g + 1) * HEAD_DIM].astype(v_ref.dtype)
    z_ref[...] = _dot(hb, w_ref[:, c0:c0 + SSD_W])
    c0 += SSD_W
    xbc_ref[...] = _dot(hb, w_ref[:, c0:c0 + CONV_W])
    c0 += CONV_W
    dt_ref[...] = _dot(hb, w_ref[:, c0:c0 + LANES])


def _mod_row(start_row, n_lat, seq, n_batch):
    return jnp.where(start_row < n_lat, start_row // seq, n_batch)


def _inproj(xs, mod, g_pre, w_in, q_gain, k_gain, cos_t, sin_t, n_batch, seq, ctx_len):
    n, d = xs.shape
    n_lat = n_batch * seq
    tm = ROW_TILE
    lat_tiles = seq // tm
    ctx_tiles = ctx_len // tm

    def tab_idx(i):
        return (jnp.where(i * tm < n_lat, i % lat_tiles, lat_tiles + (i - n_lat // tm) % ctx_tiles), 0)

    row = lambda i: (i, 0)
    const = lambda i: (0, 0)
    return pl.pallas_call(
        _inproj_kernel,
        grid=(n // tm,),
        in_specs=[pl.BlockSpec((tm, d), row),
                  pl.BlockSpec((1, 6, d), lambda i: (_mod_row(i * tm, n_lat, seq, n_batch), 0, 0)),
                  pl.BlockSpec((1, d), const),
                  pl.BlockSpec((d, IN_PAD), const),
                  pl.BlockSpec((1, LANES), const),
                  pl.BlockSpec((1, LANES), const),
                  pl.BlockSpec((tm, LANES), tab_idx),
                  pl.BlockSpec((tm, LANES), tab_idx)],
        out_specs=[pl.BlockSpec((tm, ATT_W), row),
                   pl.BlockSpec((KV_HEADS, tm, HEAD_DIM), lambda i: (0, i, 0)),
                   pl.BlockSpec((KV_HEADS, tm, HEAD_DIM), lambda i: (0, i, 0)),
                   pl.BlockSpec((tm, SSD_W), row),
                   pl.BlockSpec((tm, CONV_W), row),
                   pl.BlockSpec((tm, LANES), row)],
        out_shape=[jax.ShapeDtypeStruct((n, ATT_W), BF16),
                   jax.ShapeDtypeStruct((KV_HEADS, n, HEAD_DIM), BF16),
                   jax.ShapeDtypeStruct((KV_HEADS, n, HEAD_DIM), BF16),
                   jax.ShapeDtypeStruct((n, SSD_W), F32),
                   jax.ShapeDtypeStruct((n, CONV_W), F32),
                   jax.ShapeDtypeStruct((n, LANES), F32)],
        compiler_params=_params("arbitrary"),
        name="inproj",
    )(xs, mod, g_pre, w_in, q_gain, k_gain, cos_t, sin_t)


def _attn_heads(q_ref, kc_ref, vc_ref, kl_ref, vl_ref, o_ref, latent):
    outs = []
    q = q_ref[...]

    def scores(h):
        g = h // Q_PER_KV
        qh = q[:, h * HEAD_DIM:(h + 1) * HEAD_DIM]
        sc = lax.dot_general(qh, kc_ref[g], _NT, preferred_element_type=F32)
        sl = lax.dot_general(qh, kl_ref[g], _NT, preferred_element_type=F32) if latent else None
        return sc, sl

    ahead = scores(0)
    for h in range(HEADS):
        g = h // Q_PER_KV
        sc, sl = ahead
        if h + 1 < HEADS:
            ahead = scores(h + 1)
        m = jnp.max(sc, axis=-1, keepdims=True)
        if latent:
            m = jnp.maximum(m, jnp.max(sl, axis=-1, keepdims=True))
            pw = jnp.exp2(sl - m)
        pc = jnp.exp2(sc - m)
        den = jnp.sum(pc, axis=-1, keepdims=True)
        acc = _dot(pc.astype(BF16), vc_ref[g])
        if latent:
            den = den + jnp.sum(pw, axis=-1, keepdims=True)
            acc = acc + _dot(pw.astype(BF16), vl_ref[g])
        outs.append(acc / den)
    o_ref[...] = jnp.concatenate(outs, axis=-1).astype(o_ref.dtype)


def _attn_kernel(n_lat_q, with_ctx, q_ref, kc_ref, vc_ref, kl_ref, vl_ref, o_ref):
    if not with_ctx:
        _attn_heads(q_ref, kc_ref, vc_ref, kl_ref, vl_ref, o_ref, True)
        return
    j = pl.program_id(1)

    @pl.when(j < n_lat_q)
    def _():
        _attn_heads(q_ref, kc_ref, vc_ref, kl_ref, vl_ref, o_ref, True)

    @pl.when(j >= n_lat_q)
    def _():
        _attn_heads(q_ref, kc_ref, vc_ref, kl_ref, vl_ref, o_ref, False)


def _attention(q, k, v, n_batch, seq, ctx_len, with_ctx):
    n = q.shape[0]
    n_lat = n_batch * seq
    tq = ROW_TILE
    n_lat_q = seq // tq
    n_ctx_q = ctx_len // tq if with_ctx else 0

    def q_idx(b, j):
        return (jnp.where(j < n_lat_q, b * n_lat_q + j, n_lat // tq + b * (ctx_len // tq) + (j - n_lat_q)), 0)

    ctx_idx = lambda b, j: (0, n_lat // ctx_len + b, 0)
    lat_idx = lambda b, j: (0, b, 0)
    return pl.pallas_call(
        functools.partial(_attn_kernel, n_lat_q, with_ctx),
        grid=(n_batch, n_lat_q + n_ctx_q),
        in_specs=[pl.BlockSpec((tq, ATT_W), q_idx),
                  pl.BlockSpec((KV_HEADS, ctx_len, HEAD_DIM), ctx_idx),
                  pl.BlockSpec((KV_HEADS, ctx_len, HEAD_DIM), ctx_idx),
                  pl.BlockSpec((KV_HEADS, seq, HEAD_DIM), lat_idx),
                  pl.BlockSpec((KV_HEADS, seq, HEAD_DIM), lat_idx)],
        out_specs=pl.BlockSpec((tq, ATT_W), q_idx),
        out_shape=jax.ShapeDtypeStruct((n if with_ctx else n_lat, ATT_W), BF16),
        compiler_params=_params("arbitrary", "arbitrary"),
        name="attention",
    )(q, k, v, k, v)


def _conv_block(npc, npl, u_ref, p_ref, nx_ref, w_ref, b_ref):
    j = pl.program_id(1)
    pos = jnp.where(j < npc, j, j - npc)
    last_pos = jnp.where(j < npc, npc - 1, npl - 1)
    u = u_ref[...]
    rows = u.shape[0]
    r = lax.broadcasted_iota(jnp.int32, u.shape, 0)
    prev_row = jnp.where(pos == 0, 0.0, p_ref[SUB - 1:SUB, :])
    next_row = jnp.where(pos == last_pos, 0.0, nx_ref[0:1, :])
    up = jnp.where(r == 0, prev_row, pltpu.roll(u, 1, 0))
    dn = jnp.where(r == rows - 1, next_row, pltpu.roll(u, rows - 1, 0))
    return _silu(w_ref[0:1, :] * up + w_ref[1:2, :] * u + w_ref[2:3, :] * dn + b_ref[...])


def _ssd_chunks(direction, xcs, dts, dtb, a_neg, ex, st_ref):
    ids = range(len(xcs))
    gw = SSD_W // SSD_G
    hpg = SSD_HEADS // SSD_G
    row = lax.broadcasted_iota(jnp.int32, (CHUNK, CHUNK), 0)
    col = lax.broadcasted_iota(jnp.int32, (CHUNK, CHUNK), 1)
    tri = (col <= row) if direction == 0 else (col >= row)
    tri_b = jnp.where(tri, 1.0, 0.0).astype(BF16)
    edge = CHUNK - 1 if direction == 0 else 0

    xs = [xcs[i][:, 0:SSD_W] for i in ids]
    bgs = [[xcs[i][:, SSD_W + g * SSD_N:SSD_W + (g + 1) * SSD_N] for g in range(SSD_G)] for i in ids]
    cgs = [[xcs[i][:, SSD_W + (SSD_G + g) * SSD_N:SSD_W + (SSD_G + g + 1) * SSD_N].astype(BF16)
            for g in range(SSD_G)] for i in ids]
    grams = [[lax.dot_general(cgs[i][g], bgs[i][g].astype(BF16), _NT, preferred_element_type=F32)
              for g in range(SSD_G)] for i in ids]
    bts = [[bgs[i][g].T.astype(BF16) for g in range(SSD_G)] for i in ids]
    pre = [dts[i] + dtb for i in ids]
    dtv = [jnp.maximum(pre[i], 0.0) + jnp.log1p(jnp.exp(-jnp.abs(pre[i]))) for i in ids]
    splits = [_split3(dtv[i] * a_neg) for i in ids]
    cs = [sum(_dot(tri_b, part) for part in splits[i]) for i in ids]

    def expand(vals):
        parts = [_split2(v) for v in vals]
        return [sum(_dot(part, ex) for part in parts[i]) for i in ids]

    tot = [cs[i][edge:edge + 1, :] for i in ids]
    dt_e = expand(dtv)
    da_e = expand([jnp.exp(cs[i]) for i in ids])
    db_e = expand([jnp.exp(tot[i] - cs[i]) for i in ids])
    xd = [xs[i] * dt_e[i] for i in ids]
    xd_end = [(xd[i] * db_e[i]).astype(BF16) for i in ids]
    cs_t = [cs[i].T for i in ids]
    y_diag = [[] for _ in ids]
    for g in range(SSD_G):
        for hh in range(hpg):
            h = g * hpg + hh
            c = direction * SSD_HEADS + h
            for i in ids:
                diff = cs[i][:, c:c + 1] - cs_t[i][c:c + 1, :]
                decay = jnp.exp(jnp.where(tri, diff, NEG_BIG))
                m = (grams[i][g] * decay).astype(BF16)
                y_diag[i].append(_dot(m, xd[i][:, h * SSD_P:(h + 1) * SSD_P].astype(BF16)))
    st = st_ref[...]
    ys = []
    for i in ids:
        st_b = st.astype(BF16)
        carried = jnp.concatenate([_dot(cgs[i][g], st_b[:, g * gw:(g + 1) * gw]) for g in range(SSD_G)], axis=-1)
        ys.append(jnp.concatenate(y_diag[i], axis=-1) + carried * da_e[i])
        inject = jnp.concatenate([_dot(bts[i][g], xd_end[i][:, g * gw:(g + 1) * gw]) for g in range(SSD_G)], axis=-1)
        st = st * da_e[i][edge:edge + 1, :] + inject
    st_ref[...] = st
    return ys, xs


def _ssd_kernel(direction, npc, npl, *refs):
    if direction == 0:
        (u_ref, p_ref, nx_ref, cw_ref, cb_ref, dt_ref, dtb_ref, a_ref, ex_ref, y_ref, xc_ref, st_ref) = refs
        xc = _conv_block(npc, npl, u_ref, p_ref, nx_ref, cw_ref, cb_ref)
        xc_ref[...] = xc
    else:
        (xc_in_ref, dt_ref, dtb_ref, a_ref, ex_ref, yf_ref, z_ref, dsk_ref, gn_ref, y_ref, st_ref) = refs
        xc = xc_in_ref[...]

    @pl.when(pl.program_id(1) == 0)
    def _():
        st_ref[...] = jnp.zeros_like(st_ref)

    order = list(range(SSD_CHUNKS)) if direction == 0 else list(range(SSD_CHUNKS))[::-1]
    dt = dt_ref[...]
    xcs = [xc[c * CHUNK:(c + 1) * CHUNK] for c in order]
    dts = [dt[c * CHUNK:(c + 1) * CHUNK] for c in order]
    ys, xs = _ssd_chunks(direction, xcs, dts, dtb_ref[...], a_ref[...], ex_ref[...], st_ref)
    for c, y, x in zip(order, ys, xs):
        rows = pl.ds(c * CHUNK, CHUNK)
        if direction == 0:
            y_ref[rows, :] = y
        else:
            gated = (yf_ref[rows, :] + y + x * dsk_ref[...]) * _silu(z_ref[rows, :])
            y_ref[rows, :] = _rms(gated, gn_ref[...]).astype(y_ref.dtype)


def _ssd(direction, xin, dt, dtb, a_neg, ex, n_batch, seq, ctx_len, extra):
    n, cw = xin.shape
    n_lat = n_batch * seq
    blk = CHUNK * SSD_CHUNKS
    npc = ctx_len // blk
    npl = seq // blk
    per = blk // SUB

    def rows(b, j):
        if direction == 0:
            return jnp.where(j < npc, n_lat // blk + b * npc + j, b * npl + (j - npc))
        return jnp.where(j < npc, n_lat // blk + b * npc + (npc - 1 - j), b * npl + (npl - 1 - (j - npc)))

    const = lambda b, j: (0, 0)
    chunk = lambda width: pl.BlockSpec((blk, width), lambda b, j: (rows(b, j), 0))
    scan_specs = [chunk(LANES), pl.BlockSpec((1, LANES), const), pl.BlockSpec((1, LANES), const),
                  pl.BlockSpec((LANES, SSD_W), const)]
    if direction == 0:
        conv_w, conv_b = extra
        in_specs = [chunk(cw),
                    pl.BlockSpec((SUB, cw), lambda b, j: (jnp.maximum(rows(b, j) * per - 1, 0), 0)),
                    pl.BlockSpec((SUB, cw), lambda b, j: (jnp.minimum((rows(b, j) + 1) * per, n // SUB - 1), 0)),
                    pl.BlockSpec((3, cw), const), pl.BlockSpec((1, cw), const)] + scan_specs
        args = [xin, xin, xin, conv_w, conv_b, dt, dtb, a_neg, ex]
        out_specs = [chunk(SSD_W), chunk(cw)]
        out_shape = [jax.ShapeDtypeStruct((n, SSD_W), F32), jax.ShapeDtypeStruct((n, cw), F32)]
    else:
        yf, z, dskip, gnorm = extra
        in_specs = [chunk(cw)] + scan_specs + [chunk(SSD_W), chunk(SSD_W), pl.BlockSpec((1, SSD_W), const),
                                               pl.BlockSpec((1, SSD_W), const)]
        args = [xin, dt, dtb, a_neg, ex, yf, z, dskip, gnorm]
        out_specs = chunk(SSD_W)
        out_shape = jax.ShapeDtypeStruct((n, SSD_W), BF16)
    return pl.pallas_call(
        functools.partial(_ssd_kernel, direction, npc, npl),
        grid=(n_batch, npc + npl),
        in_specs=in_specs,
        out_specs=out_specs,
        out_shape=out_shape,
        scratch_shapes=[pltpu.VMEM((SSD_N, SSD_W), F32)],
        compiler_params=_params("arbitrary", "arbitrary"),
        name="ssd_fwd" if direction == 0 else "ssd_bwd",
    )(*args)


def _route(logits, bias):
    tokens = logits.shape[1]
    scores = _sigmoid(logits)
    choice = scores + bias
    sub = lax.broadcasted_iota(jnp.int32, (GROUP_SIZE, tokens), 0).astype(F32)
    neg = -jnp.inf

    def pick_first_max(v, idx, sentinel):
        m = jnp.max(v, axis=0, keepdims=True)
        first = jnp.min(jnp.where(v == m, idx, sentinel), axis=0, keepdims=True)
        return m, idx == first

    group_scores = []
    for g in range(N_GROUPS):
        cg = choice[g * GROUP_SIZE:(g + 1) * GROUP_SIZE, :]
        m1, hit = pick_first_max(cg, sub, GROUP_SIZE)
        m2 = jnp.max(jnp.where(hit, neg, cg), axis=0, keepdims=True)
        group_scores.append(m1 + m2)
    v = jnp.concatenate(group_scores, axis=0)
    gsel = jnp.zeros_like(v)
    for _ in range(TOPK_GROUPS):
        _, hit = pick_first_max(v, sub, N_GROUPS)
        gsel = jnp.where(hit, 1.0, gsel)
        v = jnp.where(hit, neg, v)
    ok = jnp.concatenate([jnp.broadcast_to(gsel[g:g + 1, :], (GROUP_SIZE, tokens)) for g in range(N_GROUPS)],
                         axis=0) > 0.5
    v = jnp.where(ok, choice, neg)
    eidx = lax.broadcasted_iota(jnp.int32, (N_EXP, tokens), 0).astype(F32)
    ids = []
    gates = []
    for _ in range(TOP_K):
        _, hit = pick_first_max(v, eidx, N_EXP)
        ids.append(jnp.sum(jnp.where(hit, eidx, 0.0), axis=0, keepdims=True))
        gates.append(jnp.sum(jnp.where(hit, scores, 0.0), axis=0, keepdims=True))
        v = jnp.where(hit, neg, v)
    ids = jnp.concatenate(ids, axis=0)
    gates = jnp.concatenate(gates, axis=0)
    gates = gates / jnp.sum(gates, axis=0, keepdims=True) * ROUTED_SCALE
    return ids.astype(jnp.int32), gates


def _store_slabs(slab_ref, rows):
    for kc in range(SUB):
        slab_ref[pl.ds(kc, rows.shape[0], stride=SUB), :] = rows[:, kc * LANES:(kc + 1) * LANES]


def _load_slabs(slab_ref, n):
    return jnp.concatenate([slab_ref[pl.ds(kc, n, stride=SUB), :] for kc in range(SUB)], axis=1)


def _outproj_kernel(att_ref, ssd_ref, x_ref, wo_ref, mod_ref, gpost_ref, gpre_ref, wrh_ref, wrl_ref, rb_ref,
                    x1_ref, h2_ref, h2b_ref, ids_ref, gates_ref):
    m = _dot(att_ref[...], wo_ref[0:ATT_W, :]) + _dot(ssd_ref[...], wo_ref[ATT_W:, :])
    x1 = x_ref[...] + mod_ref[0, 2:3, :] * _rms(m, gpost_ref[...])
    x1_ref[...] = x1
    h2 = _rms(x1, gpre_ref[...]) * (1.0 + mod_ref[0, 4:5, :]) + mod_ref[0, 3:4, :]
    _store_slabs(h2_ref, h2)
    h_hi, h_lo = _split2(h2)
    h2b_ref[...] = h_hi
    nt = lambda a, b: lax.dot_general(a, b, _NT, preferred_element_type=F32)
    logits = nt(wrh_ref[...], h_hi) + (nt(wrh_ref[...], h_lo) + nt(wrl_ref[...], h_hi))
    ids_ref[...], gates_ref[...] = _route(logits, rb_ref[...])


def _outproj(att, ssd, xs, w_out, mod, g_post, g_pre, wr_t, r_bias, n_rows, n_batch, seq):
    d = xs.shape[1]
    n_lat = n_batch * seq
    tm = FFN_TILE
    wr_hi = wr_t.astype(BF16)
    wr_lo = (wr_t - wr_hi.astype(F32)).astype(BF16)
    row = lambda i: (i, 0)
    const = lambda i: (0, 0)
    return pl.pallas_call(
        _outproj_kernel,
        grid=(n_rows // tm,),
        in_specs=[pl.BlockSpec((tm, ATT_W), row),
                  pl.BlockSpec((tm, SSD_W), row),
                  pl.BlockSpec((tm, d), row),
                  pl.BlockSpec((ATT_W + SSD_W, d), const),
                  pl.BlockSpec((1, 6, d), lambda i: (_mod_row(i * tm, n_lat, seq, n_batch), 0, 0)),
                  pl.BlockSpec((1, d), const),
                  pl.BlockSpec((1, d), const),
                  pl.BlockSpec((N_EXP, d), const),
                  pl.BlockSpec((N_EXP, d), const),
                  pl.BlockSpec((N_EXP, 1), const)],
        out_specs=[pl.BlockSpec((tm, d), row),
                   pl.BlockSpec((tm * SUB, LANES), row),
                   pl.BlockSpec((tm, d), row),
                   pl.BlockSpec((TOP_K, tm), lambda i: (0, i)),
                   pl.BlockSpec((TOP_K, tm), lambda i: (0, i))],
        out_shape=[jax.ShapeDtypeStruct((n_rows, d), F32),
                   jax.ShapeDtypeStruct((n_rows * SUB, LANES), F32),
                   jax.ShapeDtypeStruct((n_rows, d), BF16),
                   jax.ShapeDtypeStruct((TOP_K, n_rows), jnp.int32),
                   jax.ShapeDtypeStruct((TOP_K, n_rows), F32)],
        compiler_params=_params("arbitrary"),
        name="outproj_router",
    )(att, ssd, xs, w_out, mod, g_post, g_pre, wr_hi, wr_lo, r_bias)


def _tables_kernel(win, n_groups, ids_ref, gates_ref, list_ref, gl_ref, first_ref, ngr_ref):
    ids = ids_ref[...]
    gts = gates_ref[...]
    eidx = lax.broadcasted_iota(jnp.int32, (N_EXP, win), 0)
    hits = [ids[k:k + 1, :] == eidx for k in range(TOP_K)]
    sel = sum(jnp.where(h, 1.0, 0.0) for h in hits)
    cnt = jnp.sum(sel, axis=1, keepdims=True)
    ngr = STAGE_LISTS * jnp.floor((cnt + (STAGE_ROWS - 1)) * (1.0 / STAGE_ROWS))
    er = lax.broadcasted_iota(jnp.int32, (N_EXP, N_EXP), 0)
    ec = lax.broadcasted_iota(jnp.int32, (N_EXP, N_EXP), 1)
    below = jnp.where(ec < er, 1.0, 0.0).astype(BF16)
    ngr_b = jnp.broadcast_to(ngr, (N_EXP, LANES))
    first = _dot(below, ngr_b.astype(BF16)) + STAGE_LISTS
    cr = lax.broadcasted_iota(jnp.int32, (RANK_CHUNK, RANK_CHUNK), 0)
    cc = lax.broadcasted_iota(jnp.int32, (RANK_CHUNK, RANK_CHUNK), 1)
    before = jnp.where(cr < cc, 1.0, 0.0).astype(BF16)
    carry = jnp.zeros((N_EXP, 1), F32)
    ranks = []
    for c0 in range(0, win, RANK_CHUNK):
        s = sel[:, c0:c0 + RANK_CHUNK]
        ranks.append(_dot(s.astype(BF16), before) + carry)
        carry = carry + jnp.sum(s, axis=1, keepdims=True)
    slot_of = first[:, 0:1] * GROUP + jnp.concatenate(ranks, axis=1)
    slot = jnp.concatenate([jnp.sum(jnp.where(h, slot_of, 0.0), axis=0, keepdims=True) for h in hits], axis=0)
    hi = jnp.floor(slot * (1.0 / GROUP))
    lo = slot - hi * GROUP
    g1 = gts.astype(BF16).astype(F32)
    rest = gts - g1
    g2 = rest.astype(BF16).astype(F32)
    g3 = rest - g2
    stack = jnp.concatenate([lo, g1, g2, g3, jnp.zeros((LANES - 4 * TOP_K, win), F32)], axis=0)
    cols = stack.T
    tok1 = lax.broadcasted_iota(jnp.int32, (win, LANES), 0) + 1
    tok_a = (tok1 // TOK_SPLIT).astype(F32)
    tok_b = (tok1 % TOK_SPLIT).astype(F32)
    lane = lax.broadcasted_iota(jnp.int32, (win, LANES), 1).astype(F32)
    gidx = lax.broadcasted_iota(jnp.int32, (n_groups, win), 0).astype(F32)
    acc = jnp.zeros((n_groups, 5 * LANES), F32)
    for k in range(TOP_K):
        onehot = cols[:, k:k + 1] == lane
        vals = (tok_a, tok_b, cols[:, TOP_K + k:TOP_K + k + 1], cols[:, 2 * TOP_K + k:2 * TOP_K + k + 1],
                cols[:, 3 * TOP_K + k:3 * TOP_K + k + 1])
        rhs = jnp.concatenate([jnp.where(onehot, v, 0.0) for v in vals], axis=1).astype(BF16)
        lhs = jnp.where(hi[k:k + 1, :] == gidx, 1.0, 0.0).astype(BF16)
        acc = acc + _dot(lhs, rhs)
    tok1_tab = acc[:, 0:LANES] * TOK_SPLIT + acc[:, LANES:2 * LANES]
    gate_tab = acc[:, 2 * LANES:3 * LANES] + acc[:, 3 * LANES:4 * LANES] + acc[:, 4 * LANES:5 * LANES]
    spare = lax.broadcasted_iota(jnp.int32, (n_groups, LANES), 1).astype(F32) + win
    row_of = jnp.where(tok1_tab > 0.5, tok1_tab - 1.0, spare) * SUB
    list_ref[0] = row_of.astype(jnp.int32)
    gl_ref[0] = gate_tab
    first_ref[0] = first.astype(jnp.int32)
    ngr_ref[0] = ngr_b.astype(jnp.int32)


def _tables(ids_t, gates_t, win):
    n = ids_t.shape[1]
    n_win = n // win
    n_groups = win * TOP_K // GROUP + STAGE_LISTS * N_EXP + SUB
    blk = lambda w: (0, w)
    out = lambda w: (w, 0, 0)
    return pl.pallas_call(
        functools.partial(_tables_kernel, win, n_groups),
        grid=(n_win,),
        in_specs=[pl.BlockSpec((TOP_K, win), blk), pl.BlockSpec((TOP_K, win), blk)],
        out_specs=[pl.BlockSpec((1, n_groups, LANES), out), pl.BlockSpec((1, n_groups, LANES), out),
                   pl.BlockSpec((1, N_EXP, LANES), out), pl.BlockSpec((1, N_EXP, LANES), out)],
        out_shape=[jax.ShapeDtypeStruct((n_win, n_groups, LANES), jnp.int32),
                   jax.ShapeDtypeStruct((n_win, n_groups, LANES), F32),
                   jax.ShapeDtypeStruct((n_win, N_EXP, LANES), jnp.int32),
                   jax.ShapeDtypeStruct((n_win, N_EXP, LANES), jnp.int32)],
        compiler_params=_params("arbitrary"),
        name="dispatch_tables",
    )(ids_t, gates_t)


def _moe_kernel(win, first_ref, ngr_ref, lst_ref, gl_ref, src_ref, wg_ref, wu_ref, wd_ref, o_ref,
                buf0_ref, buf1_ref, ybuf0_ref, ybuf1_ref):
    w = pl.program_id(0)
    step = pl.program_id(1)
    acc_ref = o_ref.at[0]
    last_row = (win - 1) * SUB
    bufs = (buf0_ref, buf1_ref)
    ybufs = (ybuf0_ref, ybuf1_ref)

    def gather(g, buf_ref):
        for j in range(STAGE_LISTS):
            for r in range(GROUP):
                row = pl.multiple_of(jnp.minimum(lst_ref[0, g + j, r], last_row), SUB)
                buf_ref[pl.ds((j * GROUP + r) * SUB, SUB), :] = src_ref[pl.ds(row, SUB), :]

    def scatter(g, ybuf_ref):
        for j in range(STAGE_LISTS):
            for r0 in range(0, GROUP, SCATTER_BATCH):
                rows = [pl.multiple_of(lst_ref[0, g + j, r0 + u], SUB) for u in range(SCATTER_BATCH)]
                vals = [acc_ref[pl.ds(rows[u], SUB), :] + ybuf_ref[pl.ds((j * GROUP + r0 + u) * SUB, SUB), :]
                        for u in range(SCATTER_BATCH)]
                for u in range(SCATTER_BATCH):
                    acc_ref[pl.ds(rows[u], SUB), :] = vals[u]

    def gate_column(g):
        gate_row = jnp.broadcast_to(gl_ref[0, pl.ds(g, 1), :], (GROUP, LANES))
        diag = (lax.broadcasted_iota(jnp.int32, (GROUP, LANES), 0)
                == lax.broadcasted_iota(jnp.int32, (GROUP, LANES), 1))
        return jnp.sum(jnp.where(diag, gate_row, 0.0), axis=1, keepdims=True)

    def expert(g, k, buf_ref, ybuf_ref):
        x = _load_slabs(buf_ref, STAGE_ROWS).astype(BF16)
        a = _silu(_dot(x, wg_ref[k])) * _dot(x, wu_ref[k])
        a = a * jnp.concatenate([gate_column(g + j) for j in range(STAGE_LISTS)], axis=0)
        _store_slabs(ybuf_ref, _dot(a.astype(BF16), wd_ref[k]))

    def stage(g, k, p):
        gather(g + STAGE_LISTS, bufs[1 - p])
        expert(g, k, bufs[p], ybufs[p])
        scatter(g - STAGE_LISTS, ybufs[1 - p])

    def parity(g):
        return (g // STAGE_LISTS) % 2

    @pl.when(step == 0)
    def _():
        o_ref[...] = jnp.zeros_like(o_ref)
        ybuf0_ref[...] = jnp.zeros_like(ybuf0_ref)
        gather(STAGE_LISTS, bufs[1])

    g_end = 0
    for k in range(EXPERTS_PER_STEP):
        e = w * N_EXP + step * EXPERTS_PER_STEP + k
        g0 = first_ref[e]
        n = ngr_ref[e]
        g_end = g0 + n

        def body(i, carry, g0=g0, k=k):
            g = g0 + i * STAGE_LISTS
            for p in range(2):
                pl.when(parity(g) == p)(functools.partial(stage, g, k, p))
            return carry

        lax.fori_loop(0, n // STAGE_LISTS, body, 0)

    @pl.when(step == pl.num_programs(1) - 1)
    def _():
        g_last = g_end - STAGE_LISTS
        for p in range(2):
            pl.when(parity(g_last) == p)(functools.partial(scatter, g_last, ybufs[p]))


def _moe(h2_slabs, ids_t, gates_t, w_gate, w_up, w_down, layer, win):
    n_rows = ids_t.shape[1]
    n_win = n_rows // win
    d, fe = w_gate.shape[2:]
    lists, gate_tab, first, ngr = _tables(ids_t, gates_t, win)
    n_groups = lists.shape[1]
    slab = lambda w, s, first, ngr: (w, 0)
    tab = lambda w, s, first, ngr: (w, 0, 0)
    wspec = lambda w, s, first, ngr: (layer, s, 0, 0)
    eps = EXPERTS_PER_STEP
    return pl.pallas_call(
        functools.partial(_moe_kernel, win),
        grid_spec=pltpu.PrefetchScalarGridSpec(
            num_scalar_prefetch=2,
            grid=(n_win, N_EXP // eps),
            in_specs=[pl.BlockSpec((1, n_groups, LANES), tab, memory_space=pltpu.SMEM),
                      pl.BlockSpec((1, n_groups, LANES), tab),
                      pl.BlockSpec((win * SUB, LANES), slab),
                      pl.BlockSpec((None, eps, d, fe), wspec),
                      pl.BlockSpec((None, eps, d, fe), wspec),
                      pl.BlockSpec((None, eps, fe, d), wspec)],
            out_specs=pl.BlockSpec((1, (win + GROUP) * SUB, LANES), tab),
            scratch_shapes=[pltpu.VMEM((STAGE_ROWS * SUB, LANES), F32)] * 4),
        out_shape=jax.ShapeDtypeStruct((n_win, (win + GROUP) * SUB, LANES), F32),
        compiler_params=_params("arbitrary", "arbitrary"),
        name="moe_routed",
    )(first[:, :, 0].reshape(-1), ngr[:, :, 0].reshape(-1), lists, gate_tab, h2_slabs, w_gate, w_up, w_down)


def _ffn_out_kernel(r_ref, h_ref, x1_ref, mod_ref, gpost_ref, sg_ref, su_ref, sd_ref, o_ref):
    tm = x1_ref.shape[0]
    h = h_ref[...]
    a = _silu(_dot(h, sg_ref[...])) * _dot(h, su_ref[...])
    y = _load_slabs(r_ref.at[0], tm) + _dot(a.astype(BF16), sd_ref[...])
    o_ref[...] = x1_ref[...] + mod_ref[0, 5:6, :] * _rms(y, gpost_ref[...])


def _ffn_out(routed, h2b, x1, mod, g_post, ws_gate, ws_up, ws_down, n_batch, seq, win):
    n_rows, d = x1.shape
    n_lat = n_batch * seq
    tm = FFN_TILE
    per_win = win // tm
    fs = ws_gate.shape[-1]
    row = lambda i: (i, 0)
    const = lambda i: (0, 0)
    return pl.pallas_call(
        _ffn_out_kernel,
        grid=(n_rows // tm,),
        in_specs=[pl.BlockSpec((1, tm * SUB, LANES), lambda i: (i // per_win, i % per_win, 0)),
                  pl.BlockSpec((tm, d), row),
                  pl.BlockSpec((tm, d), row),
                  pl.BlockSpec((1, 6, d), lambda i: (_mod_row(i * tm, n_lat, seq, n_batch), 0, 0)),
                  pl.BlockSpec((1, d), const),
                  pl.BlockSpec((d, fs), const),
                  pl.BlockSpec((d, fs), const),
                  pl.BlockSpec((fs, d), const)],
        out_specs=pl.BlockSpec((tm, d), row),
        out_shape=jax.ShapeDtypeStruct((n_rows, d), F32),
        compiler_params=_params("arbitrary"),
        name="ffn_out",
    )(routed, h2b, x1, mod, g_post, ws_gate, ws_up, ws_down)


def _rope_tables(seq, ctx_len):
    rows = seq // GRID_W
    row = jnp.repeat(jnp.arange(rows), GRID_W).astype(F32)
    col = jnp.tile(jnp.arange(GRID_W), rows).astype(F32)
    n_freq = HEAD_DIM // 4
    inv_freq = ROPE_THETA ** (-jnp.arange(n_freq, dtype=F32) / n_freq)
    ang = jnp.concatenate([row[:, None] * inv_freq, col[:, None] * inv_freq], axis=-1)
    reps = LANES // (HEAD_DIM // 2)
    cos = jnp.tile(jnp.cos(ang), (1, reps))
    sign = jnp.where((jnp.arange(LANES) % HEAD_DIM) < HEAD_DIM // 2, -1.0, 1.0).astype(F32)
    sin = jnp.tile(jnp.sin(ang), (1, reps)) * sign
    cos = jnp.concatenate([cos, jnp.ones((ctx_len, LANES), F32)], axis=0)
    sin = jnp.concatenate([sin, jnp.zeros((ctx_len, LANES), F32)], axis=0)
    return cos, sin


def _pad_lanes(v):
    return jnp.pad(v.reshape(1, -1), ((0, 0), (0, LANES - v.size)))


def kernel(x, c, ctx, c_ctx, w_mod, b_mod, g_pre_mix, g_post_mix, g_pre_ffn, g_post_ffn, w_in, q_norm, k_norm,
           conv_w, conv_b, dt_bias, a_log, d_skip, ssd_norm, w_out, router_w, router_bias, w_gate, w_up, w_down,
           ws_gate, ws_up, ws_down):
    n_batch, seq, d = x.shape
    ctx_len = ctx.shape[1]
    depth = w_mod.shape[0]
    n_lat = n_batch * seq
    assert n_batch < MOD_ROWS and seq % ROW_TILE == 0 and ctx_len % ROW_TILE == 0 and seq % GRID_W == 0

    xs = jnp.concatenate([x.reshape(n_lat, d), ctx.reshape(n_batch * ctx_len, d)], axis=0)
    cvec = jnp.zeros((MOD_ROWS, d), F32).at[:n_batch].set(c).at[n_batch].set(c_ctx)
    mod_all = _modulation(cvec, w_mod, b_mod)
    cos_t, sin_t = _rope_tables(seq, ctx_len)
    head_lane = jnp.arange(LANES)[:, None]
    chan_head = jnp.arange(SSD_W)[None, :] // SSD_P
    win = math.gcd(MAX_WINDOW, math.gcd(n_lat, n_batch * ctx_len))
    w_gate_b, w_up_b, w_down_b = w_gate.astype(BF16), w_up.astype(BF16), w_down.astype(BF16)

    for i in range(depth):
        last = i == depth - 1
        mod = mod_all[i].reshape(MOD_ROWS, 6, d)
        w_in_b = jnp.pad(w_in[i].astype(BF16), ((0, 0), (0, IN_PAD - IN_W)))
        q, k, v, z, xbc, dt = _inproj(xs, mod, g_pre_mix[i].reshape(1, d), w_in_b,
                                      jnp.tile(q_norm[i], LANES // HEAD_DIM).reshape(1, LANES),
                                      jnp.tile(k_norm[i], LANES // HEAD_DIM).reshape(1, LANES),
                                      cos_t, sin_t, n_batch, seq, ctx_len)
        att = _attention(q, k, v, n_batch, seq, ctx_len, with_ctx=not last)
        dtb = _pad_lanes(dt_bias[i])
        a_neg = _pad_lanes(-jnp.exp(a_log[i]))
        y_f, xc = _ssd(0, xbc, dt, dtb, a_neg, (head_lane == chan_head).astype(BF16), n_batch, seq, ctx_len,
                       (conv_w[i], conv_b[i].reshape(1, CONV_W)))
        extra = (y_f, z, jnp.repeat(d_skip[i], SSD_P).reshape(1, SSD_W), ssd_norm[i].reshape(1, SSD_W))
        ssd = _ssd(1, xc, dt, dtb, a_neg, (head_lane == chan_head + SSD_HEADS).astype(BF16),
                   n_batch, seq, ctx_len, extra)
        n_rows = n_lat if last else xs.shape[0]
        x1, h2, h2b, ids_t, gates_t = _outproj(att, ssd, xs, w_out[i].astype(BF16), mod, g_post_mix[i].reshape(1, d),
                                          g_pre_ffn[i].reshape(1, d), router_w[i].T,
                                          router_bias[i].reshape(N_EXP, 1), n_rows, n_batch, seq)
        routed = _moe(h2, ids_t, gates_t, w_gate_b, w_up_b, w_down_b, i, win)
        xs = _ffn_out(routed, h2b, x1, mod, g_post_ffn[i].reshape(1, d), ws_gate[i].astype(BF16),
                      ws_up[i].astype(BF16), ws_down[i].astype(BF16), n_batch, seq, win)
    return xs.reshape(n_batch, seq, d)
```

```python
import functools
import math

import jax
import jax.numpy as jnp
from jax import lax
from jax.experimental import pallas as pl
from jax.experimental.pallas import tpu as pltpu

F32 = jnp.float32
BF16 = jnp.bfloat16

GRID_W = 64
HEADS = 8
KV_HEADS = 2
HEAD_DIM = 64
Q_PER_KV = HEADS // KV_HEADS
ATT_W = HEADS * HEAD_DIM
ROPE_THETA = 10000.0
SSD_HEADS = 8
SSD_P = 64
SSD_W = SSD_HEADS * SSD_P
SSD_G = 2
SSD_N = 128
CHUNK = 128
SSD_CHUNKS = 2
CONV_W = SSD_W + 2 * SSD_G * SSD_N
IN_W = ATT_W + 2 * KV_HEADS * HEAD_DIM + SSD_W + CONV_W + 2 * SSD_HEADS
LANES = 128
IN_PAD = IN_W - 2 * SSD_HEADS + LANES
N_EXP = 64
TOP_K = 8
N_GROUPS = 8
TOPK_GROUPS = 4
GROUP_SIZE = N_EXP // N_GROUPS
ROUTED_SCALE = 2.5
EPS = 1e-6
MOD_ROWS = 16
ROW_TILE = 256
MAX_WINDOW = 2048
GROUP = 128
STAGE_LISTS = 1
STAGE_ROWS = GROUP * STAGE_LISTS
SCATTER_BATCH = 16
RANK_CHUNK = 256
TOK_SPLIT = 64
FFN_TILE = 512
EXPERTS_PER_STEP = 4
SUB = 8
VMEM_LIMIT = 56 * 1024 * 1024
NEG_BIG = -1e30
LOG2_E = math.log2(math.e)

_NT = (((1,), (1,)), ((), ()))


def _params(*sem):
    return pltpu.CompilerParams(dimension_semantics=sem, vmem_limit_bytes=VMEM_LIMIT)


def _sigmoid(v):
    return 1.0 / (1.0 + jnp.exp(-v))


def _silu(v):
    return v * _sigmoid(v)


def _rms(v, gain):
    return v * lax.rsqrt(jnp.mean(v * v, axis=-1, keepdims=True) + EPS) * gain


def _dot(a, b):
    return jnp.dot(a, b, preferred_element_type=F32)


def _split2(v):
    hi = v.astype(BF16)
    lo = (v - hi.astype(F32)).astype(BF16)
    return hi, lo


def _split3(v):
    hi = v.astype(BF16)
    r = v - hi.astype(F32)
    mid = r.astype(BF16)
    lo = (r - mid.astype(F32)).astype(BF16)
    return hi, mid, lo


def _mod_kernel(c_ref, w_ref, b_ref, o_ref):
    s = _silu(c_ref[...])
    o_ref[0] = jnp.dot(s, w_ref[0], preferred_element_type=F32, precision=lax.Precision.HIGHEST) + b_ref[0]


def _modulation(cvec, w_mod, b_mod):
    depth, d, six_d = w_mod.shape
    tn = six_d // 4
    return pl.pallas_call(
        _mod_kernel,
        grid=(depth, six_d // tn),
        in_specs=[pl.BlockSpec((MOD_ROWS, d), lambda l, j: (0, 0)),
                  pl.BlockSpec((1, d, tn), lambda l, j: (l, 0, j)),
                  pl.BlockSpec((1, 1, tn), lambda l, j: (l, 0, j))],
        out_specs=pl.BlockSpec((1, MOD_ROWS, tn), lambda l, j: (l, 0, j)),
        out_shape=jax.ShapeDtypeStruct((depth, MOD_ROWS, six_d), F32),
        compiler_params=_params("arbitrary", "arbitrary"),
        name="modulation",
    )(cvec, w_mod, b_mod.reshape(depth, 1, six_d))


def _head_norm_rope(xb, gain, cos, sin):
    lane = lax.broadcasted_iota(jnp.int32, xb.shape, 1)
    low = lane < HEAD_DIM
    sq = xb * xb
    s_lo = jnp.sum(jnp.where(low, sq, 0.0), axis=-1, keepdims=True)
    s_hi = jnp.sum(jnp.where(low, 0.0, sq), axis=-1, keepdims=True)
    ms = jnp.where(low, s_lo, s_hi) * (1.0 / HEAD_DIM)
    y = xb * lax.rsqrt(ms + EPS) * gain
    ahead = pltpu.roll(y, LANES - HEAD_DIM // 2, 1)
    behind = pltpu.roll(y, HEAD_DIM // 2, 1)
    first_half = (lane % HEAD_DIM) < (HEAD_DIM // 2)
    return y * cos + jnp.where(first_half, ahead, behind) * sin


def _inproj_kernel(x_ref, mod_ref, g_ref, w_ref, qg_ref, kg_ref, cos_ref, sin_ref,
                   q_ref, k_ref, v_ref, z_ref, xbc_ref, dt_ref):
    for sub in range(x_ref.shape[0] // ROW_TILE):
        r = pl.ds(sub * ROW_TILE, ROW_TILE)
        h = _rms(x_ref[r, :], g_ref[...]) * (1.0 + mod_ref[0, 1:2, :]) + mod_ref[0, 0:1, :]
        hb = h.astype(BF16)
        cos = cos_ref[r, :]
        sin = sin_ref[r, :]
        c0 = 0
        for blk in range(ATT_W // LANES):
            qb = _dot(hb, w_ref[:, c0:c0 + LANES])
            qb = _head_norm_rope(qb, qg_ref[...], cos, sin) * (HEAD_DIM ** -0.5 * LOG2_E)
            q_ref[r, c0:c0 + LANES] = qb.astype(q_ref.dtype)
            c0 += LANES
        kb = _head_norm_rope(_dot(hb, w_ref[:, c0:c0 + LANES]), kg_ref[...], cos, sin)
        c0 += LANES
        vb = _dot(hb, w_ref[:, c0:c0 + LANES])
        c0 += LANES
        for g in range(KV_HEADS):
            k_ref[g, r, :] = kb[:, g * HEAD_DIM:(g + 1) * HEAD_DIM].astype(k_ref.dtype)
            v_ref[g, r, :] = vb[:, g * HEAD_DIM:(g + 1) * HEAD_DIM].astype(v_ref.dtype)
        z_ref[r, :] = _dot(hb, w_ref[:, c0:c0 + SSD_W])
        c0 += SSD_W
        xbc_ref[r, :] = _dot(hb, w_ref[:, c0:c0 + CONV_W])
        c0 += CONV_W
        dt_ref[r, :] = _dot(hb, w_ref[:, c0:c0 + LANES])


def _mod_row(start_row, n_lat, seq, n_batch):
    return jnp.where(start_row < n_lat, start_row // seq, n_batch)


def _inproj(xs, mod, g_pre, w_in, q_gain, k_gain, cos_t, sin_t, n_batch, seq, ctx_len):
    n, d = xs.shape
    n_lat = n_batch * seq
    tm = FFN_TILE
    lat_tiles = seq // tm

    def tab_idx(i):
        return (jnp.where(i * tm < n_lat, i % lat_tiles, lat_tiles), 0)

    row = lambda i: (i, 0)
    const = lambda i: (0, 0)
    return pl.pallas_call(
        _inproj_kernel,
        grid=(n // tm,),
        in_specs=[pl.BlockSpec((tm, d), row),
                  pl.BlockSpec((1, 6, d), lambda i: (_mod_row(i * tm, n_lat, seq, n_batch), 0, 0)),
                  pl.BlockSpec((1, d), const),
                  pl.BlockSpec((d, IN_PAD), const),
                  pl.BlockSpec((1, LANES), const),
                  pl.BlockSpec((1, LANES), const),
                  pl.BlockSpec((tm, LANES), tab_idx),
                  pl.BlockSpec((tm, LANES), tab_idx)],
        out_specs=[pl.BlockSpec((tm, ATT_W), row),
                   pl.BlockSpec((KV_HEADS, tm, HEAD_DIM), lambda i: (0, i, 0)),
                   pl.BlockSpec((KV_HEADS, tm, HEAD_DIM), lambda i: (0, i, 0)),
                   pl.BlockSpec((tm, SSD_W), row),
                   pl.BlockSpec((tm, CONV_W), row),
                   pl.BlockSpec((tm, LANES), row)],
        out_shape=[jax.ShapeDtypeStruct((n, ATT_W), BF16),
                   jax.ShapeDtypeStruct((KV_HEADS, n, HEAD_DIM), BF16),
                   jax.ShapeDtypeStruct((KV_HEADS, n, HEAD_DIM), BF16),
                   jax.ShapeDtypeStruct((n, SSD_W), F32),
                   jax.ShapeDtypeStruct((n, CONV_W), F32),
                   jax.ShapeDtypeStruct((n, LANES), F32)],
        compiler_params=_params("arbitrary"),
        name="inproj",
    )(xs, mod, g_pre, w_in, q_gain, k_gain, cos_t, sin_t)


def _attn_heads(q_ref, kc_ref, vc_ref, kl_ref, vl_ref, o_ref, latent):
    outs = []
    q = q_ref[...]

    def scores(h):
        g = h // Q_PER_KV
        qh = q[:, h * HEAD_DIM:(h + 1) * HEAD_DIM]
        sc = lax.dot_general(qh, kc_ref[g], _NT, preferred_element_type=F32)
        sl = lax.dot_general(qh, kl_ref[g], _NT, preferred_element_type=F32) if latent else None
        return sc, sl

    ahead = scores(0)
    for h in range(HEADS):
        g = h // Q_PER_KV
        sc, sl = ahead
        if h + 1 < HEADS:
            ahead = scores(h + 1)
        m = jnp.max(sc, axis=-1, keepdims=True)
        if latent:
            m = jnp.maximum(m, jnp.max(sl, axis=-1, keepdims=True))
            pw = jnp.exp2(sl - m)
        pc = jnp.exp2(sc - m)
        den = jnp.sum(pc, axis=-1, keepdims=True)
        acc = _dot(pc.astype(BF16), vc_ref[g])
        if latent:
            den = den + jnp.sum(pw, axis=-1, keepdims=True)
            acc = acc + _dot(pw.astype(BF16), vl_ref[g])
        outs.append(acc / den)
    o_ref[...] = jnp.concatenate(outs, axis=-1).astype(o_ref.dtype)


def _attn_kernel(n_lat_q, with_ctx, q_ref, kc_ref, vc_ref, kl_ref, vl_ref, o_ref):
    if not with_ctx:
        _attn_heads(q_ref, kc_ref, vc_ref, kl_ref, vl_ref, o_ref, True)
        return
    j = pl.program_id(1)

    @pl.when(j < n_lat_q)
    def _():
        _attn_heads(q_ref, kc_ref, vc_ref, kl_ref, vl_ref, o_ref, True)

    @pl.when(j >= n_lat_q)
    def _():
        _attn_heads(q_ref, kc_ref, vc_ref, kl_ref, vl_ref, o_ref, False)


def _attention(q, k, v, n_batch, seq, ctx_len, with_ctx):
    n = q.shape[0]
    n_lat = n_batch * seq
    tq = ROW_TILE
    n_lat_q = seq // tq
    n_ctx_q = ctx_len // tq if with_ctx else 0

    def q_idx(b, j):
        return (jnp.where(j < n_lat_q, b * n_lat_q + j, n_lat // tq + b * (ctx_len // tq) + (j - n_lat_q)), 0)

    ctx_idx = lambda b, j: (0, n_lat // ctx_len + b, 0)
    lat_idx = lambda b, j: (0, b, 0)
    return pl.pallas_call(
        functools.partial(_attn_kernel, n_lat_q, with_ctx),
        grid=(n_batch, n_lat_q + n_ctx_q),
        in_specs=[pl.BlockSpec((tq, ATT_W), q_idx),
                  pl.BlockSpec((KV_HEADS, ctx_len, HEAD_DIM), ctx_idx),
                  pl.BlockSpec((KV_HEADS, ctx_len, HEAD_DIM), ctx_idx),
                  pl.BlockSpec((KV_HEADS, seq, HEAD_DIM), lat_idx),
                  pl.BlockSpec((KV_HEADS, seq, HEAD_DIM), lat_idx)],
        out_specs=pl.BlockSpec((tq, ATT_W), q_idx),
        out_shape=jax.ShapeDtypeStruct((n if with_ctx else n_lat, ATT_W), BF16),
        compiler_params=_params("arbitrary", "arbitrary"),
        name="attention",
    )(q, k, v, k, v)


def _conv_block(npc, npl, u_ref, p_ref, nx_ref, w_ref, b_ref):
    j = pl.program_id(1)
    pos = jnp.where(j < npc, j, j - npc)
    last_pos = jnp.where(j < npc, npc - 1, npl - 1)
    u = u_ref[...]
    rows = u.shape[0]
    r = lax.broadcasted_iota(jnp.int32, u.shape, 0)
    prev_row = jnp.where(pos == 0, 0.0, p_ref[SUB - 1:SUB, :])
    next_row = jnp.where(pos == last_pos, 0.0, nx_ref[0:1, :])
    up = jnp.where(r == 0, prev_row, pltpu.roll(u, 1, 0))
    dn = jnp.where(r == rows - 1, next_row, pltpu.roll(u, rows - 1, 0))
    return _silu(w_ref[0:1, :] * up + w_ref[1:2, :] * u + w_ref[2:3, :] * dn + b_ref[...])


def _ssd_chunks(direction, xcs, dts, dtb, a_neg, ex, st_ref):
    ids = range(len(xcs))
    gw = SSD_W // SSD_G
    hpg = SSD_HEADS // SSD_G
    row = lax.broadcasted_iota(jnp.int32, (CHUNK, CHUNK), 0)
    col = lax.broadcasted_iota(jnp.int32, (CHUNK, CHUNK), 1)
    tri = (col <= row) if direction == 0 else (col >= row)
    tri_b = jnp.where(tri, 1.0, 0.0).astype(BF16)
    edge = CHUNK - 1 if direction == 0 else 0

    xs = [xcs[i][:, 0:SSD_W] for i in ids]
    bgs = [[xcs[i][:, SSD_W + g * SSD_N:SSD_W + (g + 1) * SSD_N] for g in range(SSD_G)] for i in ids]
    cgs = [[xcs[i][:, SSD_W + (SSD_G + g) * SSD_N:SSD_W + (SSD_G + g + 1) * SSD_N].astype(BF16)
            for g in range(SSD_G)] for i in ids]
    grams = [[lax.dot_general(cgs[i][g], bgs[i][g].astype(BF16), _NT, preferred_element_type=F32)
              for g in range(SSD_G)] for i in ids]
    bts = [[bgs[i][g].T.astype(BF16) for g in range(SSD_G)] for i in ids]
    pre = [dts[i] + dtb for i in ids]
    dtv = [jnp.maximum(pre[i], 0.0) + jnp.log1p(jnp.exp(-jnp.abs(pre[i]))) for i in ids]
    splits = [_split3(dtv[i] * a_neg) for i in ids]
    cs = [sum(_dot(tri_b, part) for part in splits[i]) for i in ids]

    def expand(vals):
        parts = [_split2(v) for v in vals]
        return [sum(_dot(part, ex) for part in parts[i]) for i in ids]

    tot = [cs[i][edge:edge + 1, :] for i in ids]
    dt_e = expand(dtv)
    da_e = expand([jnp.exp(cs[i]) for i in ids])
    db_e = expand([jnp.exp(tot[i] - cs[i]) for i in ids])
    xd = [xs[i] * dt_e[i] for i in ids]
    xd_end = [(xd[i] * db_e[i]).astype(BF16) for i in ids]
    cs_t = [cs[i].T for i in ids]
    y_diag = [[] for _ in ids]
    for g in range(SSD_G):
        for hh in range(hpg):
            h = g * hpg + hh
            c = direction * SSD_HEADS + h
            for i in ids:
                diff = cs[i][:, c:c + 1] - cs_t[i][c:c + 1, :]
                decay = jnp.exp(jnp.where(tri, diff, NEG_BIG))
                m = (grams[i][g] * decay).astype(BF16)
                y_diag[i].append(_dot(m, xd[i][:, h * SSD_P:(h + 1) * SSD_P].astype(BF16)))
    st = st_ref[...]
    ys = []
    for i in ids:
        st_b = st.astype(BF16)
        carried = jnp.concatenate([_dot(cgs[i][g], st_b[:, g * gw:(g + 1) * gw]) for g in range(SSD_G)], axis=-1)
        ys.append(jnp.concatenate(y_diag[i], axis=-1) + carried * da_e[i])
        inject = jnp.concatenate([_dot(bts[i][g], xd_end[i][:, g * gw:(g + 1) * gw]) for g in range(SSD_G)], axis=-1)
        st = st * da_e[i][edge:edge + 1, :] + inject
    st_ref[...] = st
    return ys, xs


def _ssd_kernel(direction, npc, npl, *refs):
    if direction == 0:
        (u_ref, p_ref, nx_ref, cw_ref, cb_ref, dt_ref, dtb_ref, a_ref, ex_ref, y_ref, xc_ref, st_ref) = refs
        xc = _conv_block(npc, npl, u_ref, p_ref, nx_ref, cw_ref, cb_ref)
        xc_ref[...] = xc
    else:
        (xc_in_ref, dt_ref, dtb_ref, a_ref, ex_ref, yf_ref, z_ref, dsk_ref, gn_ref, y_ref, st_ref) = refs
        xc = xc_in_ref[...]

    @pl.when(pl.program_id(1) == 0)
    def _():
        st_ref[...] = jnp.zeros_like(st_ref)

    order = list(range(SSD_CHUNKS)) if direction == 0 else list(range(SSD_CHUNKS))[::-1]
    dt = dt_ref[...]
    xcs = [xc[c * CHUNK:(c + 1) * CHUNK] for c in order]
    dts = [dt[c * CHUNK:(c + 1) * CHUNK] for c in order]
    ys, xs = _ssd_chunks(direction, xcs, dts, dtb_ref[...], a_ref[...], ex_ref[...], st_ref)
    for c, y, x in zip(order, ys, xs):
        rows = pl.ds(c * CHUNK, CHUNK)
        if direction == 0:
            y_ref[rows, :] = y
        else:
            gated = (yf_ref[rows, :] + y + x * dsk_ref[...]) * _silu(z_ref[rows, :])
            y_ref[rows, :] = _rms(gated, gn_ref[...]).astype(y_ref.dtype)


def _ssd(direction, xin, dt, dtb, a_neg, ex, n_batch, seq, ctx_len, extra):
    n, cw = xin.shape
    n_lat = n_batch * seq
    blk = CHUNK * SSD_CHUNKS
    npc = ctx_len // blk
    npl = seq // blk
    per = blk // SUB

    def rows(b, j):
        if direction == 0:
            return jnp.where(j < npc, n_lat // blk + b * npc + j, b * npl + (j - npc))
        return jnp.where(j < npc, n_lat // blk + b * npc + (npc - 1 - j), b * npl + (npl - 1 - (j - npc)))

    const = lambda b, j: (0, 0)
    chunk = lambda width: pl.BlockSpec((blk, width), lambda b, j: (rows(b, j), 0))
    scan_specs = [chunk(LANES), pl.BlockSpec((1, LANES), const), pl.BlockSpec((1, LANES), const),
                  pl.BlockSpec((LANES, SSD_W), const)]
    if direction == 0:
        conv_w, conv_b = extra
        in_specs = [chunk(cw),
                    pl.BlockSpec((SUB, cw), lambda b, j: (jnp.maximum(rows(b, j) * per - 1, 0), 0)),
                    pl.BlockSpec((SUB, cw), lambda b, j: (jnp.minimum((rows(b, j) + 1) * per, n // SUB - 1), 0)),
                    pl.BlockSpec((3, cw), const), pl.BlockSpec((1, cw), const)] + scan_specs
        args = [xin, xin, xin, conv_w, conv_b, dt, dtb, a_neg, ex]
        out_specs = [chunk(SSD_W), chunk(cw)]
        out_shape = [jax.ShapeDtypeStruct((n, SSD_W), F32), jax.ShapeDtypeStruct((n, cw), F32)]
    else:
        yf, z, dskip, gnorm = extra
        in_specs = [chunk(cw)] + scan_specs + [chunk(SSD_W), chunk(SSD_W), pl.BlockSpec((1, SSD_W), const),
                                               pl.BlockSpec((1, SSD_W), const)]
        args = [xin, dt, dtb, a_neg, ex, yf, z, dskip, gnorm]
        out_specs = chunk(SSD_W)
        out_shape = jax.ShapeDtypeStruct((n, SSD_W), BF16)
    return pl.pallas_call(
        functools.partial(_ssd_kernel, direction, npc, npl),
        grid=(n_batch, npc + npl),
        in_specs=in_specs,
        out_specs=out_specs,
        out_shape=out_shape,
        scratch_shapes=[pltpu.VMEM((SSD_N, SSD_W), F32)],
        compiler_params=_params("arbitrary", "arbitrary"),
        name="ssd_fwd" if direction == 0 else "ssd_bwd",
    )(*args)


def _route(logits, bias):
    tokens = logits.shape[1]
    scores = _sigmoid(logits)
    choice = scores + bias
    sub = lax.broadcasted_iota(jnp.int32, (GROUP_SIZE, tokens), 0).astype(F32)
    neg = -jnp.inf

    def pick_first_max(v, idx, sentinel):
        m = jnp.max(v, axis=0, keepdims=True)
        first = jnp.min(jnp.where(v == m, idx, sentinel), axis=0, keepdims=True)
        return m, idx == first

    group_scores = []
    for g in range(N_GROUPS):
        cg = choice[g * GROUP_SIZE:(g + 1) * GROUP_SIZE, :]
        m1, hit = pick_first_max(cg, sub, GROUP_SIZE)
        m2 = jnp.max(jnp.where(hit, neg, cg), axis=0, keepdims=True)
        group_scores.append(m1 + m2)
    v = jnp.concatenate(group_scores, axis=0)
    gsel = jnp.zeros_like(v)
    for _ in range(TOPK_GROUPS):
        _, hit = pick_first_max(v, sub, N_GROUPS)
        gsel = jnp.where(hit, 1.0, gsel)
        v = jnp.where(hit, neg, v)
    ok = jnp.concatenate([jnp.broadcast_to(gsel[g:g + 1, :], (GROUP_SIZE, tokens)) for g in range(N_GROUPS)],
                         axis=0) > 0.5
    v = jnp.where(ok, choice, neg)
    eidx = lax.broadcasted_iota(jnp.int32, (N_EXP, tokens), 0).astype(F32)
    ids = []
    gates = []
    for _ in range(TOP_K):
        _, hit = pick_first_max(v, eidx, N_EXP)
        ids.append(jnp.sum(jnp.where(hit, eidx, 0.0), axis=0, keepdims=True))
        gates.append(jnp.sum(jnp.where(hit, scores, 0.0), axis=0, keepdims=True))
        v = jnp.where(hit, neg, v)
    ids = jnp.concatenate(ids, axis=0)
    gates = jnp.concatenate(gates, axis=0)
    gates = gates / jnp.sum(gates, axis=0, keepdims=True) * ROUTED_SCALE
    return ids.astype(jnp.int32), gates


def _store_slabs(slab_ref, rows):
    for kc in range(SUB):
        slab_ref[pl.ds(kc, rows.shape[0], stride=SUB), :] = rows[:, kc * LANES:(kc + 1) * LANES]


def _load_slabs(slab_ref, n):
    return jnp.concatenate([slab_ref[pl.ds(kc, n, stride=SUB), :] for kc in range(SUB)], axis=1)


def _outproj_kernel(att_ref, ssd_ref, x_ref, wo_ref, mod_ref, gpost_ref, gpre_ref, wrh_ref, wrl_ref, rb_ref,
                    x1_ref, h2_ref, h2b_ref, ids_ref, gates_ref):
    m = _dot(att_ref[...], wo_ref[0:ATT_W, :]) + _dot(ssd_ref[...], wo_ref[ATT_W:, :])
    x1 = x_ref[...] + mod_ref[0, 2:3, :] * _rms(m, gpost_ref[...])
    x1_ref[...] = x1
    h2 = _rms(x1, gpre_ref[...]) * (1.0 + mod_ref[0, 4:5, :]) + mod_ref[0, 3:4, :]
    _store_slabs(h2_ref, h2)
    h_hi, h_lo = _split2(h2)
    h2b_ref[...] = h_hi
    nt = lambda a, b: lax.dot_general(a, b, _NT, preferred_element_type=F32)
    logits = nt(wrh_ref[...], h_hi) + (nt(wrh_ref[...], h_lo) + nt(wrl_ref[...], h_hi))
    ids_ref[...], gates_ref[...] = _route(logits, rb_ref[...])


def _outproj(att, ssd, xs, w_out, mod, g_post, g_pre, wr_t, r_bias, n_rows, n_batch, seq):
    d = xs.shape[1]
    n_lat = n_batch * seq
    tm = FFN_TILE
    wr_hi = wr_t.astype(BF16)
    wr_lo = (wr_t - wr_hi.astype(F32)).astype(BF16)
    row = lambda i: (i, 0)
    const = lambda i: (0, 0)
    return pl.pallas_call(
        _outproj_kernel,
        grid=(n_rows // tm,),
        in_specs=[pl.BlockSpec((tm, ATT_W), row),
                  pl.BlockSpec((tm, SSD_W), row),
                  pl.BlockSpec((tm, d), row),
                  pl.BlockSpec((ATT_W + SSD_W, d), const),
                  pl.BlockSpec((1, 6, d), lambda i: (_mod_row(i * tm, n_lat, seq, n_batch), 0, 0)),
                  pl.BlockSpec((1, d), const),
                  pl.BlockSpec((1, d), const),
                  pl.BlockSpec((N_EXP, d), const),
                  pl.BlockSpec((N_EXP, d), const),
                  pl.BlockSpec((N_EXP, 1), const)],
        out_specs=[pl.BlockSpec((tm, d), row),
                   pl.BlockSpec((tm * SUB, LANES), row),
                   pl.BlockSpec((tm, d), row),
                   pl.BlockSpec((TOP_K, tm), lambda i: (0, i)),
                   pl.BlockSpec((TOP_K, tm), lambda i: (0, i))],
        out_shape=[jax.ShapeDtypeStruct((n_rows, d), F32),
                   jax.ShapeDtypeStruct((n_rows * SUB, LANES), F32),
                   jax.ShapeDtypeStruct((n_rows, d), BF16),
                   jax.ShapeDtypeStruct((TOP_K, n_rows), jnp.int32),
                   jax.ShapeDtypeStruct((TOP_K, n_rows), F32)],
        compiler_params=_params("arbitrary"),
        name="outproj_router",
    )(att, ssd, xs, w_out, mod, g_post, g_pre, wr_hi, wr_lo, r_bias)


def _tables_kernel(win, n_groups, ids_ref, gates_ref, list_ref, gl_ref, first_ref, ngr_ref):
    ids = ids_ref[...]
    gts = gates_ref[...]
    eidx = lax.broadcasted_iota(jnp.int32, (N_EXP, win), 0)
    hits = [ids[k:k + 1, :] == eidx for k in range(TOP_K)]
    sel = sum(jnp.where(h, 1.0, 0.0) for h in hits)
    cnt = jnp.sum(sel, axis=1, keepdims=True)
    ngr = STAGE_LISTS * jnp.floor((cnt + (STAGE_ROWS - 1)) * (1.0 / STAGE_ROWS))
    er = lax.broadcasted_iota(jnp.int32, (N_EXP, N_EXP), 0)
    ec = lax.broadcasted_iota(jnp.int32, (N_EXP, N_EXP), 1)
    below = jnp.where(ec < er, 1.0, 0.0).astype(BF16)
    ngr_b = jnp.broadcast_to(ngr, (N_EXP, LANES))
    first = _dot(below, ngr_b.astype(BF16)) + STAGE_LISTS
    cr = lax.broadcasted_iota(jnp.int32, (RANK_CHUNK, RANK_CHUNK), 0)
    cc = lax.broadcasted_iota(jnp.int32, (RANK_CHUNK, RANK_CHUNK), 1)
    before = jnp.where(cr < cc, 1.0, 0.0).astype(BF16)
    carry = jnp.zeros((N_EXP, 1), F32)
    ranks = []
    for c0 in range(0, win, RANK_CHUNK):
        s = sel[:, c0:c0 + RANK_CHUNK]
        ranks.append(_dot(s.astype(BF16), before) + carry)
        carry = carry + jnp.sum(s, axis=1, keepdims=True)
    slot_of = first[:, 0:1] * GROUP + jnp.concatenate(ranks, axis=1)
    slot = jnp.concatenate([jnp.sum(jnp.where(h, slot_of, 0.0), axis=0, keepdims=True) for h in hits], axis=0)
    hi = jnp.floor(slot * (1.0 / GROUP))
    lo = slot - hi * GROUP
    g1 = gts.astype(BF16).astype(F32)
    rest = gts - g1
    g2 = rest.astype(BF16).astype(F32)
    g3 = rest - g2
    stack = jnp.concatenate([lo, g1, g2, g3, jnp.zeros((LANES - 4 * TOP_K, win), F32)], axis=0)
    cols = stack.T
    tok1 = lax.broadcasted_iota(jnp.int32, (win, LANES), 0) + 1
    tok_a = (tok1 // TOK_SPLIT).astype(F32)
    tok_b = (tok1 % TOK_SPLIT).astype(F32)
    lane = lax.broadcasted_iota(jnp.int32, (win, LANES), 1).astype(F32)
    gidx = lax.broadcasted_iota(jnp.int32, (n_groups, win), 0).astype(F32)
    acc = jnp.zeros((n_groups, 5 * LANES), F32)
    for k in range(TOP_K):
        onehot = cols[:, k:k + 1] == lane
        vals = (tok_a, tok_b, cols[:, TOP_K + k:TOP_K + k + 1], cols[:, 2 * TOP_K + k:2 * TOP_K + k + 1],
                cols[:, 3 * TOP_K + k:3 * TOP_K + k + 1])
        rhs = jnp.concatenate([jnp.where(onehot, v, 0.0) for v in vals], axis=1).astype(BF16)
        lhs = jnp.where(hi[k:k + 1, :] == gidx, 1.0, 0.0).astype(BF16)
        acc = acc + _dot(lhs, rhs)
    tok1_tab = acc[:, 0:LANES] * TOK_SPLIT + acc[:, LANES:2 * LANES]
    gate_tab = acc[:, 2 * LANES:3 * LANES] + acc[:, 3 * LANES:4 * LANES] + acc[:, 4 * LANES:5 * LANES]
    spare = lax.broadcasted_iota(jnp.int32, (n_groups, LANES), 1).astype(F32) + win
    row_of = jnp.where(tok1_tab > 0.5, tok1_tab - 1.0, spare) * SUB
    list_ref[0] = row_of.astype(jnp.int32)
    gl_ref[0] = gate_tab
    first_ref[0] = first.astype(jnp.int32)
    ngr_ref[0] = ngr_b.astype(jnp.int32)


def _tables(ids_t, gates_t, win):
    n = ids_t.shape[1]
    n_win = n // win
    n_groups = win * TOP_K // GROUP + STAGE_LISTS * N_EXP + SUB
    blk = lambda w: (0, w)
    out = lambda w: (w, 0, 0)
    return pl.pallas_call(
        functools.partial(_tables_kernel, win, n_groups),
        grid=(n_win,),
        in_specs=[pl.BlockSpec((TOP_K, win), blk), pl.BlockSpec((TOP_K, win), blk)],
        out_specs=[pl.BlockSpec((1, n_groups, LANES), out), pl.BlockSpec((1, n_groups, LANES), out),
                   pl.BlockSpec((1, N_EXP, LANES), out), pl.BlockSpec((1, N_EXP, LANES), out)],
        out_shape=[jax.ShapeDtypeStruct((n_win, n_groups, LANES), jnp.int32),
                   jax.ShapeDtypeStruct((n_win, n_groups, LANES), F32),
                   jax.ShapeDtypeStruct((n_win, N_EXP, LANES), jnp.int32),
                   jax.ShapeDtypeStruct((n_win, N_EXP, LANES), jnp.int32)],
        compiler_params=_params("arbitrary"),
        name="dispatch_tables",
    )(ids_t, gates_t)


def _moe_kernel(win, first_ref, ngr_ref, lst_ref, gl_ref, src_ref, wg_ref, wu_ref, wd_ref, o_ref,
                buf0_ref, buf1_ref, ybuf0_ref, ybuf1_ref):
    w = pl.program_id(0)
    step = pl.program_id(1)
    acc_ref = o_ref.at[0]
    last_row = (win - 1) * SUB
    bufs = (buf0_ref, buf1_ref)
    ybufs = (ybuf0_ref, ybuf1_ref)

    def gather(g, buf_ref):
        for j in range(STAGE_LISTS):
            for r in range(GROUP):
                row = pl.multiple_of(jnp.minimum(lst_ref[0, g + j, r], last_row), SUB)
                buf_ref[pl.ds((j * GROUP + r) * SUB, SUB), :] = src_ref[pl.ds(row, SUB), :]

    def scatter(g, ybuf_ref):
        for j in range(STAGE_LISTS):
            for r0 in range(0, GROUP, SCATTER_BATCH):
                rows = [pl.multiple_of(lst_ref[0, g + j, r0 + u], SUB) for u in range(SCATTER_BATCH)]
                vals = [acc_ref[pl.ds(rows[u], SUB), :] + ybuf_ref[pl.ds((j * GROUP + r0 + u) * SUB, SUB), :]
                        for u in range(SCATTER_BATCH)]
                for u in range(SCATTER_BATCH):
                    acc_ref[pl.ds(rows[u], SUB), :] = vals[u]

    def gate_column(g):
        gate_row = jnp.broadcast_to(gl_ref[0, pl.ds(g, 1), :], (GROUP, LANES))
        diag = (lax.broadcasted_iota(jnp.int32, (GROUP, LANES), 0)
                == lax.broadcasted_iota(jnp.int32, (GROUP, LANES), 1))
        return jnp.sum(jnp.where(diag, gate_row, 0.0), axis=1, keepdims=True)

    def expert(g, k, buf_ref, ybuf_ref):
        x = _load_slabs(buf_ref, STAGE_ROWS).astype(BF16)
        a = _silu(_dot(x, wg_ref[k])) * _dot(x, wu_ref[k])
        a = a * jnp.concatenate([gate_column(g + j) for j in range(STAGE_LISTS)], axis=0)
        _store_slabs(ybuf_ref, _dot(a.astype(BF16), wd_ref[k]))

    def stage(g, k, p):
        gather(g + STAGE_LISTS, bufs[1 - p])
        expert(g, k, bufs[p], ybufs[p])
        scatter(g - STAGE_LISTS, ybufs[1 - p])

    def parity(g):
        return (g // STAGE_LISTS) % 2

    @pl.when(step == 0)
    def _():
        o_ref[...] = jnp.zeros_like(o_ref)
        ybuf0_ref[...] = jnp.zeros_like(ybuf0_ref)
        gather(STAGE_LISTS, bufs[1])

    g_end = 0
    for k in range(EXPERTS_PER_STEP):
        e = w * N_EXP + step * EXPERTS_PER_STEP + k
        g0 = first_ref[e]
        n = ngr_ref[e]
        g_end = g0 + n

        def body(i, carry, g0=g0, k=k):
            g = g0 + i * STAGE_LISTS
            for p in range(2):
                pl.when(parity(g) == p)(functools.partial(stage, g, k, p))
            return carry

        lax.fori_loop(0, n // STAGE_LISTS, body, 0)

    @pl.when(step == pl.num_programs(1) - 1)
    def _():
        g_last = g_end - STAGE_LISTS
        for p in range(2):
            pl.when(parity(g_last) == p)(functools.partial(scatter, g_last, ybufs[p]))


def _moe(h2_slabs, ids_t, gates_t, w_gate, w_up, w_down, layer, win):
    n_rows = ids_t.shape[1]
    n_win = n_rows // win
    d, fe = w_gate.shape[2:]
    lists, gate_tab, first, ngr = _tables(ids_t, gates_t, win)
    n_groups = lists.shape[1]
    slab = lambda w, s, first, ngr: (w, 0)
    tab = lambda w, s, first, ngr: (w, 0, 0)
    wspec = lambda w, s, first, ngr: (layer, s, 0, 0)
    eps = EXPERTS_PER_STEP
    return pl.pallas_call(
        functools.partial(_moe_kernel, win),
        grid_spec=pltpu.PrefetchScalarGridSpec(
            num_scalar_prefetch=2,
            grid=(n_win, N_EXP // eps),
            in_specs=[pl.BlockSpec((1, n_groups, LANES), tab, memory_space=pltpu.SMEM),
                      pl.BlockSpec((1, n_groups, LANES), tab),
                      pl.BlockSpec((win * SUB, LANES), slab),
                      pl.BlockSpec((None, eps, d, fe), wspec),
                      pl.BlockSpec((None, eps, d, fe), wspec),
                      pl.BlockSpec((None, eps, fe, d), wspec)],
            out_specs=pl.BlockSpec((1, (win + GROUP) * SUB, LANES), tab),
            scratch_shapes=[pltpu.VMEM((STAGE_ROWS * SUB, LANES), F32)] * 4),
        out_shape=jax.ShapeDtypeStruct((n_win, (win + GROUP) * SUB, LANES), F32),
        compiler_params=_params("arbitrary", "arbitrary"),
        name="moe_routed",
    )(first[:, :, 0].reshape(-1), ngr[:, :, 0].reshape(-1), lists, gate_tab, h2_slabs, w_gate, w_up, w_down)


def _ffn_out_kernel(r_ref, h_ref, x1_ref, mod_ref, gpost_ref, sg_ref, su_ref, sd_ref, o_ref):
    tm = x1_ref.shape[0]
    h = h_ref[...]
    a = _silu(_dot(h, sg_ref[...])) * _dot(h, su_ref[...])
    y = _load_slabs(r_ref.at[0], tm) + _dot(a.astype(BF16), sd_ref[...])
    o_ref[...] = x1_ref[...] + mod_ref[0, 5:6, :] * _rms(y, gpost_ref[...])


def _ffn_out(routed, h2b, x1, mod, g_post, ws_gate, ws_up, ws_down, n_batch, seq, win):
    n_rows, d = x1.shape
    n_lat = n_batch * seq
    tm = FFN_TILE
    per_win = win // tm
    fs = ws_gate.shape[-1]
    row = lambda i: (i, 0)
    const = lambda i: (0, 0)
    return pl.pallas_call(
        _ffn_out_kernel,
        grid=(n_rows // tm,),
        in_specs=[pl.BlockSpec((1, tm * SUB, LANES), lambda i: (i // per_win, i % per_win, 0)),
                  pl.BlockSpec((tm, d), row),
                  pl.BlockSpec((tm, d), row),
                  pl.BlockSpec((1, 6, d), lambda i: (_mod_row(i * tm, n_lat, seq, n_batch), 0, 0)),
                  pl.BlockSpec((1, d), const),
                  pl.BlockSpec((d, fs), const),
                  pl.BlockSpec((d, fs), const),
                  pl.BlockSpec((fs, d), const)],
        out_specs=pl.BlockSpec((tm, d), row),
        out_shape=jax.ShapeDtypeStruct((n_rows, d), F32),
        compiler_params=_params("arbitrary"),
        name="ffn_out",
    )(routed, h2b, x1, mod, g_post, ws_gate, ws_up, ws_down)


def _rope_tables(seq, ctx_len):
    rows = seq // GRID_W
    row = jnp.repeat(jnp.arange(rows), GRID_W).astype(F32)
    col = jnp.tile(jnp.arange(GRID_W), rows).astype(F32)
    n_freq = HEAD_DIM // 4
    inv_freq = ROPE_THETA ** (-jnp.arange(n_freq, dtype=F32) / n_freq)
    ang = jnp.concatenate([row[:, None] * inv_freq, col[:, None] * inv_freq], axis=-1)
    reps = LANES // (HEAD_DIM // 2)
    cos = jnp.tile(jnp.cos(ang), (1, reps))
    sign = jnp.where((jnp.arange(LANES) % HEAD_DIM) < HEAD_DIM // 2, -1.0, 1.0).astype(F32)
    sin = jnp.tile(jnp.sin(ang), (1, reps)) * sign
    cos = jnp.concatenate([cos, jnp.ones((FFN_TILE, LANES), F32)], axis=0)
    sin = jnp.concatenate([sin, jnp.zeros((FFN_TILE, LANES), F32)], axis=0)
    return cos, sin


def _pad_lanes(v):
    return jnp.pad(v.reshape(1, -1), ((0, 0), (0, LANES - v.size)))


def kernel(x, c, ctx, c_ctx, w_mod, b_mod, g_pre_mix, g_post_mix, g_pre_ffn, g_post_ffn, w_in, q_norm, k_norm,
           conv_w, conv_b, dt_bias, a_log, d_skip, ssd_norm, w_out, router_w, router_bias, w_gate, w_up, w_down,
           ws_gate, ws_up, ws_down):
    n_batch, seq, d = x.shape
    ctx_len = ctx.shape[1]
    depth = w_mod.shape[0]
    n_lat = n_batch * seq
    assert n_batch < MOD_ROWS and seq % ROW_TILE == 0 and ctx_len % ROW_TILE == 0 and seq % GRID_W == 0

    xs = jnp.concatenate([x.reshape(n_lat, d), ctx.reshape(n_batch * ctx_len, d)], axis=0)
    cvec = jnp.zeros((MOD_ROWS, d), F32).at[:n_batch].set(c).at[n_batch].set(c_ctx)
    mod_all = _modulation(cvec, w_mod, b_mod)
    cos_t, sin_t = _rope_tables(seq, ctx_len)
    head_lane = jnp.arange(LANES)[:, None]
    chan_head = jnp.arange(SSD_W)[None, :] // SSD_P
    win = math.gcd(MAX_WINDOW, math.gcd(n_lat, n_batch * ctx_len))
    w_gate_b, w_up_b, w_down_b = w_gate.astype(BF16), w_up.astype(BF16), w_down.astype(BF16)

    for i in range(depth):
        last = i == depth - 1
        mod = mod_all[i].reshape(MOD_ROWS, 6, d)
        w_in_b = jnp.pad(w_in[i].astype(BF16), ((0, 0), (0, IN_PAD - IN_W)))
        q, k, v, z, xbc, dt = _inproj(xs, mod, g_pre_mix[i].reshape(1, d), w_in_b,
                                      jnp.tile(q_norm[i], LANES // HEAD_DIM).reshape(1, LANES),
                                      jnp.tile(k_norm[i], LANES // HEAD_DIM).reshape(1, LANES),
                                      cos_t, sin_t, n_batch, seq, ctx_len)
        att = _attention(q, k, v, n_batch, seq, ctx_len, with_ctx=not last)
        dtb = _pad_lanes(dt_bias[i])
        a_neg = _pad_lanes(-jnp.exp(a_log[i]))
        y_f, xc = _ssd(0, xbc, dt, dtb, a_neg, (head_lane == chan_head).astype(BF16), n_batch, seq, ctx_len,
                       (conv_w[i], conv_b[i].reshape(1, CONV_W)))
        extra = (y_f, z, jnp.repeat(d_skip[i], SSD_P).reshape(1, SSD_W), ssd_norm[i].reshape(1, SSD_W))
        ssd = _ssd(1, xc, dt, dtb, a_neg, (head_lane == chan_head + SSD_HEADS).astype(BF16),
                   n_batch, seq, ctx_len, extra)
        n_rows = n_lat if last else xs.shape[0]
        x1, h2, h2b, ids_t, gates_t = _outproj(att, ssd, xs, w_out[i].astype(BF16), mod, g_post_mix[i].reshape(1, d),
                                          g_pre_ffn[i].reshape(1, d), router_w[i].T,
                                          router_bias[i].reshape(N_EXP, 1), n_rows, n_batch, seq)
        routed = _moe(h2, ids_t, gates_t, w_gate_b, w_up_b, w_down_b, i, win)
        xs = _ffn_out(routed, h2b, x1, mod, g_post_ffn[i].reshape(1, d), ws_gate[i].astype(BF16),
                      ws_up[i].astype(BF16), ws_down[i].astype(BF16), n_batch, seq, win)
    return xs.reshape(n_batch, seq, d)
```

```python
import functools
import math

import jax
import jax.numpy as jnp
from jax import lax
from jax.experimental import pallas as pl
from jax.experimental.pallas import tpu as pltpu

F32 = jnp.float32
BF16 = jnp.bfloat16

GRID_W = 64
HEADS = 8
KV_HEADS = 2
HEAD_DIM = 64
Q_PER_KV = HEADS // KV_HEADS
ATT_W = HEADS * HEAD_DIM
ROPE_THETA = 10000.0
SSD_HEADS = 8
SSD_P = 64
SSD_W = SSD_HEADS * SSD_P
SSD_G = 2
SSD_N = 128
CHUNK = 128
SSD_CHUNKS = 2
CONV_W = SSD_W + 2 * SSD_G * SSD_N
IN_W = ATT_W + 2 * KV_HEADS * HEAD_DIM + SSD_W + CONV_W + 2 * SSD_HEADS
LANES = 128
IN_PAD = IN_W - 2 * SSD_HEADS + LANES
N_EXP = 64
TOP_K = 8
N_GROUPS = 8
TOPK_GROUPS = 4
GROUP_SIZE = N_EXP // N_GROUPS
ROUTED_SCALE = 2.5
EPS = 1e-6
MOD_ROWS = 16
ROW_TILE = 256
ATTN_LOOKAHEAD = 1
MAX_WINDOW = 2048
GROUP = 128
STAGE_LISTS = 1
STAGE_ROWS = GROUP * STAGE_LISTS
SCATTER_BATCH = 16
RANK_CHUNK = 256
TOK_SPLIT = 64
FFN_TILE = 512
ROUTER_TILE = 1024
EXPERTS_PER_STEP = 4
SUB = 8
VMEM_LIMIT = 56 * 1024 * 1024
NEG_BIG = -1e30
LOG2_E = math.log2(math.e)

_NT = (((1,), (1,)), ((), ()))


def _params(*sem):
    return pltpu.CompilerParams(dimension_semantics=sem, vmem_limit_bytes=VMEM_LIMIT)


def _sigmoid(v):
    return 1.0 / (1.0 + jnp.exp(-v))


def _silu(v):
    return v * _sigmoid(v)


def _rms(v, gain):
    return v * lax.rsqrt(jnp.mean(v * v, axis=-1, keepdims=True) + EPS) * gain


def _dot(a, b):
    return jnp.dot(a, b, preferred_element_type=F32)


def _split2(v):
    hi = v.astype(BF16)
    lo = (v - hi.astype(F32)).astype(BF16)
    return hi, lo


def _split3(v):
    hi = v.astype(BF16)
    r = v - hi.astype(F32)
    mid = r.astype(BF16)
    lo = (r - mid.astype(F32)).astype(BF16)
    return hi, mid, lo


def _mod_kernel(c_ref, w_ref, b_ref, o_ref):
    s = _silu(c_ref[...])
    o_ref[0] = jnp.dot(s, w_ref[0], preferred_element_type=F32, precision=lax.Precision.HIGHEST) + b_ref[0]


def _modulation(cvec, w_mod, b_mod):
    depth, d, six_d = w_mod.shape
    tn = six_d // 4
    return pl.pallas_call(
        _mod_kernel,
        grid=(depth, six_d // tn),
        in_specs=[pl.BlockSpec((MOD_ROWS, d), lambda l, j: (0, 0)),
                  pl.BlockSpec((1, d, tn), lambda l, j: (l, 0, j)),
                  pl.BlockSpec((1, 1, tn), lambda l, j: (l, 0, j))],
        out_specs=pl.BlockSpec((1, MOD_ROWS, tn), lambda l, j: (l, 0, j)),
        out_shape=jax.ShapeDtypeStruct((depth, MOD_ROWS, six_d), F32),
        compiler_params=_params("arbitrary", "arbitrary"),
        name="modulation",
    )(cvec, w_mod, b_mod.reshape(depth, 1, six_d))


def _head_norm_rope(xb, gain, cos, sin):
    lane = lax.broadcasted_iota(jnp.int32, xb.shape, 1)
    low = lane < HEAD_DIM
    sq = xb * xb
    s_lo = jnp.sum(jnp.where(low, sq, 0.0), axis=-1, keepdims=True)
    s_hi = jnp.sum(jnp.where(low, 0.0, sq), axis=-1, keepdims=True)
    ms = jnp.where(low, s_lo, s_hi) * (1.0 / HEAD_DIM)
    y = xb * lax.rsqrt(ms + EPS) * gain
    ahead = pltpu.roll(y, LANES - HEAD_DIM // 2, 1)
    behind = pltpu.roll(y, HEAD_DIM // 2, 1)
    first_half = (lane % HEAD_DIM) < (HEAD_DIM // 2)
    return y * cos + jnp.where(first_half, ahead, behind) * sin


def _inproj_kernel(x_ref, mod_ref, g_ref, w_ref, qg_ref, kg_ref, cos_ref, sin_ref,
                   q_ref, k_ref, v_ref, z_ref, xbc_ref, dt_ref):
    for sub in range(x_ref.shape[0] // ROW_TILE):
        r = pl.ds(sub * ROW_TILE, ROW_TILE)
        h = _rms(x_ref[r, :], g_ref[...]) * (1.0 + mod_ref[0, 1:2, :]) + mod_ref[0, 0:1, :]
        hb = h.astype(BF16)
        cos = cos_ref[r, :]
        sin = sin_ref[r, :]
        c0 = 0
        for blk in range(ATT_W // LANES):
            qb = _dot(hb, w_ref[:, c0:c0 + LANES])
            qb = _head_norm_rope(qb, qg_ref[...], cos, sin) * (HEAD_DIM ** -0.5 * LOG2_E)
            q_ref[r, c0:c0 + LANES] = qb.astype(q_ref.dtype)
            c0 += LANES
        kb = _head_norm_rope(_dot(hb, w_ref[:, c0:c0 + LANES]), kg_ref[...], cos, sin)
        c0 += LANES
        vb = _dot(hb, w_ref[:, c0:c0 + LANES])
        c0 += LANES
        for g in range(KV_HEADS):
            k_ref[g, r, :] = kb[:, g * HEAD_DIM:(g + 1) * HEAD_DIM].astype(k_ref.dtype)
            v_ref[g, r, :] = vb[:, g * HEAD_DIM:(g + 1) * HEAD_DIM].astype(v_ref.dtype)
        z_ref[r, :] = _dot(hb, w_ref[:, c0:c0 + SSD_W])
        c0 += SSD_W
        xbc_ref[r, :] = _dot(hb, w_ref[:, c0:c0 + CONV_W])
        c0 += CONV_W
        dt_ref[r, :] = _dot(hb, w_ref[:, c0:c0 + LANES])


def _mod_row(start_row, n_lat, seq, n_batch):
    return jnp.where(start_row < n_lat, start_row // seq, n_batch)


def _inproj(xs, mod, g_pre, w_in, q_gain, k_gain, cos_t, sin_t, n_batch, seq):
    n, d = xs.shape
    n_lat = n_batch * seq
    tm = FFN_TILE
    lat_tiles = seq // tm

    def tab_idx(i):
        return (jnp.where(i * tm < n_lat, i % lat_tiles, lat_tiles), 0)

    row = lambda i: (i, 0)
    const = lambda i: (0, 0)
    return pl.pallas_call(
        _inproj_kernel,
        grid=(n // tm,),
        in_specs=[pl.BlockSpec((tm, d), row),
                  pl.BlockSpec((1, 6, d), lambda i: (_mod_row(i * tm, n_lat, seq, n_batch), 0, 0)),
                  pl.BlockSpec((1, d), const),
                  pl.BlockSpec((d, IN_PAD), const),
                  pl.BlockSpec((1, LANES), const),
                  pl.BlockSpec((1, LANES), const),
                  pl.BlockSpec((tm, LANES), tab_idx),
                  pl.BlockSpec((tm, LANES), tab_idx)],
        out_specs=[pl.BlockSpec((tm, ATT_W), row),
                   pl.BlockSpec((KV_HEADS, tm, HEAD_DIM), lambda i: (0, i, 0)),
                   pl.BlockSpec((KV_HEADS, tm, HEAD_DIM), lambda i: (0, i, 0)),
                   pl.BlockSpec((tm, SSD_W), row),
                   pl.BlockSpec((tm, CONV_W), row),
                   pl.BlockSpec((tm, LANES), row)],
        out_shape=[jax.ShapeDtypeStruct((n, ATT_W), BF16),
                   jax.ShapeDtypeStruct((KV_HEADS, n, HEAD_DIM), BF16),
                   jax.ShapeDtypeStruct((KV_HEADS, n, HEAD_DIM), BF16),
                   jax.ShapeDtypeStruct((n, SSD_W), F32),
                   jax.ShapeDtypeStruct((n, CONV_W), F32),
                   jax.ShapeDtypeStruct((n, LANES), F32)],
        compiler_params=_params("arbitrary"),
        name="inproj",
    )(xs, mod, g_pre, w_in, q_gain, k_gain, cos_t, sin_t)


def _attn_heads(q_ref, kc_ref, vc_ref, kl_ref, vl_ref, o_ref, latent):
    outs = []
    q = q_ref[...]

    def scores(h):
        g = h // Q_PER_KV
        qh = q[:, h * HEAD_DIM:(h + 1) * HEAD_DIM]
        sc = lax.dot_general(qh, kc_ref[g], _NT, preferred_element_type=F32)
        sl = lax.dot_general(qh, kl_ref[g], _NT, preferred_element_type=F32) if latent else None
        return sc, sl

    ahead = [scores(h) for h in range(ATTN_LOOKAHEAD)]
    for h in range(HEADS):
        g = h // Q_PER_KV
        sc, sl = ahead.pop(0)
        if h + ATTN_LOOKAHEAD < HEADS:
            ahead.append(scores(h + ATTN_LOOKAHEAD))
        m = jnp.max(sc, axis=-1, keepdims=True)
        if latent:
            m = jnp.maximum(m, jnp.max(sl, axis=-1, keepdims=True))
            pw = jnp.exp2(sl - m)
        pc = jnp.exp2(sc - m)
        den = jnp.sum(pc, axis=-1, keepdims=True)
        acc = _dot(pc.astype(BF16), vc_ref[g])
        if latent:
            den = den + jnp.sum(pw, axis=-1, keepdims=True)
            acc = acc + _dot(pw.astype(BF16), vl_ref[g])
        outs.append(acc / den)
    o_ref[...] = jnp.concatenate(outs, axis=-1).astype(o_ref.dtype)


def _attn_kernel(n_lat_q, with_ctx, q_ref, kc_ref, vc_ref, kl_ref, vl_ref, o_ref):
    if not with_ctx:
        _attn_heads(q_ref, kc_ref, vc_ref, kl_ref, vl_ref, o_ref, True)
        return
    j = pl.program_id(1)

    @pl.when(j < n_lat_q)
    def _():
        _attn_heads(q_ref, kc_ref, vc_ref, kl_ref, vl_ref, o_ref, True)

    @pl.when(j >= n_lat_q)
    def _():
        _attn_heads(q_ref, kc_ref, vc_ref, kl_ref, vl_ref, o_ref, False)


def _attention(q, k, v, n_batch, seq, ctx_len, with_ctx):
    n = q.shape[0]
    n_lat = n_batch * seq
    tq = ROW_TILE
    n_lat_q = seq // tq
    n_ctx_q = ctx_len // tq if with_ctx else 0

    def q_idx(b, j):
        return (jnp.where(j < n_lat_q, b * n_lat_q + j, n_lat // tq + b * (ctx_len // tq) + (j - n_lat_q)), 0)

    ctx_idx = lambda b, j: (0, n_lat // ctx_len + b, 0)
    lat_idx = lambda b, j: (0, b, 0)
    return pl.pallas_call(
        functools.partial(_attn_kernel, n_lat_q, with_ctx),
        grid=(n_batch, n_lat_q + n_ctx_q),
        in_specs=[pl.BlockSpec((tq, ATT_W), q_idx),
                  pl.BlockSpec((KV_HEADS, ctx_len, HEAD_DIM), ctx_idx),
                  pl.BlockSpec((KV_HEADS, ctx_len, HEAD_DIM), ctx_idx),
                  pl.BlockSpec((KV_HEADS, seq, HEAD_DIM), lat_idx),
                  pl.BlockSpec((KV_HEADS, seq, HEAD_DIM), lat_idx)],
        out_specs=pl.BlockSpec((tq, ATT_W), q_idx),
        out_shape=jax.ShapeDtypeStruct((n if with_ctx else n_lat, ATT_W), BF16),
        compiler_params=_params("arbitrary", "arbitrary"),
        name="attention",
    )(q, k, v, k, v)


def _conv_block(npc, npl, u_ref, p_ref, nx_ref, w_ref, b_ref):
    j = pl.program_id(1)
    pos = jnp.where(j < npc, j, j - npc)
    last_pos = jnp.where(j < npc, npc - 1, npl - 1)
    u = u_ref[...]
    rows = u.shape[0]
    r = lax.broadcasted_iota(jnp.int32, u.shape, 0)
    prev_row = jnp.where(pos == 0, 0.0, p_ref[SUB - 1:SUB, :])
    next_row = jnp.where(pos == last_pos, 0.0, nx_ref[0:1, :])
    up = jnp.where(r == 0, prev_row, pltpu.roll(u, 1, 0))
    dn = jnp.where(r == rows - 1, next_row, pltpu.roll(u, rows - 1, 0))
    return _silu(w_ref[0:1, :] * up + w_ref[1:2, :] * u + w_ref[2:3, :] * dn + b_ref[...])


def _ssd_chunks(direction, xcs, dts, dtb, a_neg, ex, st_ref):
    ids = range(len(xcs))
    gw = SSD_W // SSD_G
    hpg = SSD_HEADS // SSD_G
    row = lax.broadcasted_iota(jnp.int32, (CHUNK, CHUNK), 0)
    col = lax.broadcasted_iota(jnp.int32, (CHUNK, CHUNK), 1)
    tri = (col <= row) if direction == 0 else (col >= row)
    tri_b = jnp.where(tri, 1.0, 0.0).astype(BF16)
    edge = CHUNK - 1 if direction == 0 else 0

    xs = [xcs[i][:, 0:SSD_W] for i in ids]
    bgs = [[xcs[i][:, SSD_W + g * SSD_N:SSD_W + (g + 1) * SSD_N] for g in range(SSD_G)] for i in ids]
    cgs = [[xcs[i][:, SSD_W + (SSD_G + g) * SSD_N:SSD_W + (SSD_G + g + 1) * SSD_N].astype(BF16)
            for g in range(SSD_G)] for i in ids]
    grams = [[lax.dot_general(cgs[i][g], bgs[i][g].astype(BF16), _NT, preferred_element_type=F32)
              for g in range(SSD_G)] for i in ids]
    bts = [[bgs[i][g].T.astype(BF16) for g in range(SSD_G)] for i in ids]
    pre = [dts[i] + dtb for i in ids]
    dtv = [jnp.maximum(pre[i], 0.0) + jnp.log1p(jnp.exp(-jnp.abs(pre[i]))) for i in ids]
    splits = [_split3(dtv[i] * a_neg) for i in ids]
    cs = [sum(_dot(tri_b, part) for part in splits[i]) for i in ids]

    def expand(vals):
        parts = [_split2(v) for v in vals]
        return [sum(_dot(part, ex) for part in parts[i]) for i in ids]

    tot = [cs[i][edge:edge + 1, :] for i in ids]
    dt_e = expand(dtv)
    da_e = expand([jnp.exp(cs[i]) for i in ids])
    db_e = expand([jnp.exp(tot[i] - cs[i]) for i in ids])
    xd = [xs[i] * dt_e[i] for i in ids]
    xd_end = [(xd[i] * db_e[i]).astype(BF16) for i in ids]
    cs_t = [cs[i].T for i in ids]
    y_diag = [[] for _ in ids]
    for g in range(SSD_G):
        for hh in range(hpg):
            h = g * hpg + hh
            c = direction * SSD_HEADS + h
            for i in ids:
                diff = cs[i][:, c:c + 1] - cs_t[i][c:c + 1, :]
                decay = jnp.exp(jnp.where(tri, diff, NEG_BIG))
                m = (grams[i][g] * decay).astype(BF16)
                y_diag[i].append(_dot(m, xd[i][:, h * SSD_P:(h + 1) * SSD_P].astype(BF16)))
    st = st_ref[...]
    ys = []
    for i in ids:
        st_b = st.astype(BF16)
        carried = jnp.concatenate([_dot(cgs[i][g], st_b[:, g * gw:(g + 1) * gw]) for g in range(SSD_G)], axis=-1)
        ys.append(jnp.concatenate(y_diag[i], axis=-1) + carried * da_e[i])
        inject = jnp.concatenate([_dot(bts[i][g], xd_end[i][:, g * gw:(g + 1) * gw]) for g in range(SSD_G)], axis=-1)
        st = st * da_e[i][edge:edge + 1, :] + inject
    st_ref[...] = st
    return ys, xs


def _ssd_kernel(direction, npc, npl, *refs):
    if direction == 0:
        (u_ref, p_ref, nx_ref, cw_ref, cb_ref, dt_ref, dtb_ref, a_ref, ex_ref, y_ref, xc_ref, st_ref) = refs
        xc = _conv_block(npc, npl, u_ref, p_ref, nx_ref, cw_ref, cb_ref)
        xc_ref[...] = xc
    else:
        (xc_in_ref, dt_ref, dtb_ref, a_ref, ex_ref, yf_ref, z_ref, dsk_ref, gn_ref, y_ref, st_ref) = refs
        xc = xc_in_ref[...]

    @pl.when(pl.program_id(1) == 0)
    def _():
        st_ref[...] = jnp.zeros_like(st_ref)

    order = list(range(SSD_CHUNKS)) if direction == 0 else list(range(SSD_CHUNKS))[::-1]
    dt = dt_ref[...]
    xcs = [xc[c * CHUNK:(c + 1) * CHUNK] for c in order]
    dts = [dt[c * CHUNK:(c + 1) * CHUNK] for c in order]
    ys, xs = _ssd_chunks(direction, xcs, dts, dtb_ref[...], a_ref[...], ex_ref[...], st_ref)
    for c, y, x in zip(order, ys, xs):
        rows = pl.ds(c * CHUNK, CHUNK)
        if direction == 0:
            y_ref[rows, :] = y
        else:
            gated = (yf_ref[rows, :] + y + x * dsk_ref[...]) * _silu(z_ref[rows, :])
            y_ref[rows, :] = _rms(gated, gn_ref[...]).astype(y_ref.dtype)


def _ssd(direction, xin, dt, dtb, a_neg, ex, n_batch, seq, ctx_len, extra):
    n, cw = xin.shape
    n_lat = n_batch * seq
    blk = CHUNK * SSD_CHUNKS
    npc = ctx_len // blk
    npl = seq // blk
    per = blk // SUB

    def rows(b, j):
        if direction == 0:
            return jnp.where(j < npc, n_lat // blk + b * npc + j, b * npl + (j - npc))
        return jnp.where(j < npc, n_lat // blk + b * npc + (npc - 1 - j), b * npl + (npl - 1 - (j - npc)))

    const = lambda b, j: (0, 0)
    chunk = lambda width: pl.BlockSpec((blk, width), lambda b, j: (rows(b, j), 0))
    scan_specs = [chunk(LANES), pl.BlockSpec((1, LANES), const), pl.BlockSpec((1, LANES), const),
                  pl.BlockSpec((LANES, SSD_W), const)]
    if direction == 0:
        conv_w, conv_b = extra
        in_specs = [chunk(cw),
                    pl.BlockSpec((SUB, cw), lambda b, j: (jnp.maximum(rows(b, j) * per - 1, 0), 0)),
                    pl.BlockSpec((SUB, cw), lambda b, j: (jnp.minimum((rows(b, j) + 1) * per, n // SUB - 1), 0)),
                    pl.BlockSpec((3, cw), const), pl.BlockSpec((1, cw), const)] + scan_specs
        args = [xin, xin, xin, conv_w, conv_b, dt, dtb, a_neg, ex]
        out_specs = [chunk(SSD_W), chunk(cw)]
        out_shape = [jax.ShapeDtypeStruct((n, SSD_W), F32), jax.ShapeDtypeStruct((n, cw), F32)]
    else:
        yf, z, dskip, gnorm = extra
        in_specs = [chunk(cw)] + scan_specs + [chunk(SSD_W), chunk(SSD_W), pl.BlockSpec((1, SSD_W), const),
                                               pl.BlockSpec((1, SSD_W), const)]
        args = [xin, dt, dtb, a_neg, ex, yf, z, dskip, gnorm]
        out_specs = chunk(SSD_W)
        out_shape = jax.ShapeDtypeStruct((n, SSD_W), BF16)
    return pl.pallas_call(
        functools.partial(_ssd_kernel, direction, npc, npl),
        grid=(n_batch, npc + npl),
        in_specs=in_specs,
        out_specs=out_specs,
        out_shape=out_shape,
        scratch_shapes=[pltpu.VMEM((SSD_N, SSD_W), F32)],
        compiler_params=_params("arbitrary", "arbitrary"),
        name="ssd_fwd" if direction == 0 else "ssd_bwd",
    )(*args)


def _route(logits, bias):
    tokens = logits.shape[1]
    scores = _sigmoid(logits)
    choice = scores + bias
    sub = lax.broadcasted_iota(jnp.int32, (GROUP_SIZE, tokens), 0).astype(F32)
    neg = -jnp.inf

    def pick_first_max(v, idx, sentinel):
        m = jnp.max(v, axis=0, keepdims=True)
        first = jnp.min(jnp.where(v == m, idx, sentinel), axis=0, keepdims=True)
        return m, idx == first

    group_scores = []
    for g in range(N_GROUPS):
        cg = choice[g * GROUP_SIZE:(g + 1) * GROUP_SIZE, :]
        m1, hit = pick_first_max(cg, sub, GROUP_SIZE)
        m2 = jnp.max(jnp.where(hit, neg, cg), axis=0, keepdims=True)
        group_scores.append(m1 + m2)
    v = jnp.concatenate(group_scores, axis=0)
    gsel = jnp.zeros_like(v)
    for _ in range(TOPK_GROUPS):
        _, hit = pick_first_max(v, sub, N_GROUPS)
        gsel = jnp.where(hit, 1.0, gsel)
        v = jnp.where(hit, neg, v)
    ok = jnp.concatenate([jnp.broadcast_to(gsel[g:g + 1, :], (GROUP_SIZE, tokens)) for g in range(N_GROUPS)],
                         axis=0) > 0.5
    v = jnp.where(ok, choice, neg)
    eidx = lax.broadcasted_iota(jnp.int32, (N_EXP, tokens), 0).astype(F32)
    ids = []
    gates = []
    for _ in range(TOP_K):
        _, hit = pick_first_max(v, eidx, N_EXP)
        ids.append(jnp.sum(jnp.where(hit, eidx, 0.0), axis=0, keepdims=True))
        gates.append(jnp.sum(jnp.where(hit, scores, 0.0), axis=0, keepdims=True))
        v = jnp.where(hit, neg, v)
    ids = jnp.concatenate(ids, axis=0)
    gates = jnp.concatenate(gates, axis=0)
    gates = gates / jnp.sum(gates, axis=0, keepdims=True) * ROUTED_SCALE
    return ids.astype(jnp.int32), gates


def _store_slabs(slab_ref, rows):
    for kc in range(SUB):
        slab_ref[pl.ds(kc, rows.shape[0], stride=SUB), :] = rows[:, kc * LANES:(kc + 1) * LANES]


def _load_slabs(slab_ref, n):
    return jnp.concatenate([slab_ref[pl.ds(kc, n, stride=SUB), :] for kc in range(SUB)], axis=1)


def _outproj_kernel(att_ref, ssd_ref, x_ref, wo_ref, mod_ref, gpost_ref, gpre_ref, wrh_ref, wrl_ref, rb_ref,
                    x1_ref, h2_ref, h2b_ref, ids_ref, gates_ref):
    m = _dot(att_ref[...], wo_ref[0:ATT_W, :]) + _dot(ssd_ref[...], wo_ref[ATT_W:, :])
    x1 = x_ref[...] + mod_ref[0, 2:3, :] * _rms(m, gpost_ref[...])
    x1_ref[...] = x1
    h2 = _rms(x1, gpre_ref[...]) * (1.0 + mod_ref[0, 4:5, :]) + mod_ref[0, 3:4, :]
    _store_slabs(h2_ref, h2)
    h_hi, h_lo = _split2(h2)
    h2b_ref[...] = h_hi
    nt = lambda a, b: lax.dot_general(a, b, _NT, preferred_element_type=F32)
    logits = nt(wrh_ref[...], h_hi) + (nt(wrh_ref[...], h_lo) + nt(wrl_ref[...], h_hi))
    ids_ref[...], gates_ref[...] = _route(logits, rb_ref[...])


def _outproj(att, ssd, xs, w_out, mod, g_post, g_pre, wr_t, r_bias, n_rows, n_batch, seq):
    d = xs.shape[1]
    n_lat = n_batch * seq
    tm = math.gcd(ROUTER_TILE, math.gcd(seq, n_rows))
    wr_hi = wr_t.astype(BF16)
    wr_lo = (wr_t - wr_hi.astype(F32)).astype(BF16)
    row = lambda i: (i, 0)
    const = lambda i: (0, 0)
    return pl.pallas_call(
        _outproj_kernel,
        grid=(n_rows // tm,),
        in_specs=[pl.BlockSpec((tm, ATT_W), row),
                  pl.BlockSpec((tm, SSD_W), row),
                  pl.BlockSpec((tm, d), row),
                  pl.BlockSpec((ATT_W + SSD_W, d), const),
                  pl.BlockSpec((1, 6, d), lambda i: (_mod_row(i * tm, n_lat, seq, n_batch), 0, 0)),
                  pl.BlockSpec((1, d), const),
                  pl.BlockSpec((1, d), const),
                  pl.BlockSpec((N_EXP, d), const),
                  pl.BlockSpec((N_EXP, d), const),
                  pl.BlockSpec((N_EXP, 1), const)],
        out_specs=[pl.BlockSpec((tm, d), row),
                   pl.BlockSpec((tm * SUB, LANES), row),
                   pl.BlockSpec((tm, d), row),
                   pl.BlockSpec((TOP_K, tm), lambda i: (0, i)),
                   pl.BlockSpec((TOP_K, tm), lambda i: (0, i))],
        out_shape=[jax.ShapeDtypeStruct((n_rows, d), F32),
                   jax.ShapeDtypeStruct((n_rows * SUB, LANES), F32),
                   jax.ShapeDtypeStruct((n_rows, d), BF16),
                   jax.ShapeDtypeStruct((TOP_K, n_rows), jnp.int32),
                   jax.ShapeDtypeStruct((TOP_K, n_rows), F32)],
        compiler_params=_params("arbitrary"),
        name="outproj_router",
    )(att, ssd, xs, w_out, mod, g_post, g_pre, wr_hi, wr_lo, r_bias)


def _tables_kernel(win, n_groups, ids_ref, gates_ref, list_ref, gl_ref, first_ref, ngr_ref):
    ids = ids_ref[...]
    gts = gates_ref[...]
    eidx = lax.broadcasted_iota(jnp.int32, (N_EXP, win), 0)
    hits = [ids[k:k + 1, :] == eidx for k in range(TOP_K)]
    sel = sum(jnp.where(h, 1.0, 0.0) for h in hits)
    cnt = jnp.sum(sel, axis=1, keepdims=True)
    ngr = STAGE_LISTS * jnp.floor((cnt + (STAGE_ROWS - 1)) * (1.0 / STAGE_ROWS))
    er = lax.broadcasted_iota(jnp.int32, (N_EXP, N_EXP), 0)
    ec = lax.broadcasted_iota(jnp.int32, (N_EXP, N_EXP), 1)
    below = jnp.where(ec < er, 1.0, 0.0).astype(BF16)
    ngr_b = jnp.broadcast_to(ngr, (N_EXP, LANES))
    first = _dot(below, ngr_b.astype(BF16)) + STAGE_LISTS
    cr = lax.broadcasted_iota(jnp.int32, (RANK_CHUNK, RANK_CHUNK), 0)
    cc = lax.broadcasted_iota(jnp.int32, (RANK_CHUNK, RANK_CHUNK), 1)
    before = jnp.where(cr < cc, 1.0, 0.0).astype(BF16)
    carry = jnp.zeros((N_EXP, 1), F32)
    ranks = []
    for c0 in range(0, win, RANK_CHUNK):
        s = sel[:, c0:c0 + RANK_CHUNK]
        ranks.append(_dot(s.astype(BF16), before) + carry)
        carry = carry + jnp.sum(s, axis=1, keepdims=True)
    slot_of = first[:, 0:1] * GROUP + jnp.concatenate(ranks, axis=1)
    slot = jnp.concatenate([jnp.sum(jnp.where(h, slot_of, 0.0), axis=0, keepdims=True) for h in hits], axis=0)
    hi = jnp.floor(slot * (1.0 / GROUP))
    lo = slot - hi * GROUP
    g1 = gts.astype(BF16).astype(F32)
    rest = gts - g1
    g2 = rest.astype(BF16).astype(F32)
    g3 = rest - g2
    stack = jnp.concatenate([lo, g1, g2, g3, jnp.zeros((LANES - 4 * TOP_K, win), F32)], axis=0)
    cols = stack.T
    tok1 = lax.broadcasted_iota(jnp.int32, (win, LANES), 0) + 1
    tok_a = (tok1 // TOK_SPLIT).astype(F32)
    tok_b = (tok1 % TOK_SPLIT).astype(F32)
    lane = lax.broadcasted_iota(jnp.int32, (win, LANES), 1).astype(F32)
    gidx = lax.broadcasted_iota(jnp.int32, (n_groups, win), 0).astype(F32)
    acc = jnp.zeros((n_groups, 5 * LANES), F32)
    for k in range(TOP_K):
        onehot = cols[:, k:k + 1] == lane
        vals = (tok_a, tok_b, cols[:, TOP_K + k:TOP_K + k + 1], cols[:, 2 * TOP_K + k:2 * TOP_K + k + 1],
                cols[:, 3 * TOP_K + k:3 * TOP_K + k + 1])
        rhs = jnp.concatenate([jnp.where(onehot, v, 0.0) for v in vals], axis=1).astype(BF16)
        lhs = jnp.where(hi[k:k + 1, :] == gidx, 1.0, 0.0).astype(BF16)
        acc = acc + _dot(lhs, rhs)
    tok1_tab = acc[:, 0:LANES] * TOK_SPLIT + acc[:, LANES:2 * LANES]
    gate_tab = acc[:, 2 * LANES:3 * LANES] + acc[:, 3 * LANES:4 * LANES] + acc[:, 4 * LANES:5 * LANES]
    spare = lax.broadcasted_iota(jnp.int32, (n_groups, LANES), 1).astype(F32) + win
    row_of = jnp.where(tok1_tab > 0.5, tok1_tab - 1.0, spare) * SUB
    list_ref[0] = row_of.astype(jnp.int32)
    gl_ref[0] = gate_tab
    first_ref[0] = first.astype(jnp.int32)
    ngr_ref[0] = ngr_b.astype(jnp.int32)


def _tables(ids_t, gates_t, win):
    n = ids_t.shape[1]
    n_win = n // win
    n_groups = win * TOP_K // GROUP + STAGE_LISTS * N_EXP + SUB
    blk = lambda w: (0, w)
    out = lambda w: (w, 0, 0)
    return pl.pallas_call(
        functools.partial(_tables_kernel, win, n_groups),
        grid=(n_win,),
        in_specs=[pl.BlockSpec((TOP_K, win), blk), pl.BlockSpec((TOP_K, win), blk)],
        out_specs=[pl.BlockSpec((1, n_groups, LANES), out), pl.BlockSpec((1, n_groups, LANES), out),
                   pl.BlockSpec((1, N_EXP, LANES), out), pl.BlockSpec((1, N_EXP, LANES), out)],
        out_shape=[jax.ShapeDtypeStruct((n_win, n_groups, LANES), jnp.int32),
                   jax.ShapeDtypeStruct((n_win, n_groups, LANES), F32),
                   jax.ShapeDtypeStruct((n_win, N_EXP, LANES), jnp.int32),
                   jax.ShapeDtypeStruct((n_win, N_EXP, LANES), jnp.int32)],
        compiler_params=_params("arbitrary"),
        name="dispatch_tables",
    )(ids_t, gates_t)


def _moe_kernel(win, first_ref, ngr_ref, lst_ref, gl_ref, src_ref, wg_ref, wu_ref, wd_ref, o_ref,
                buf0_ref, buf1_ref, ybuf0_ref, ybuf1_ref):
    w = pl.program_id(0)
    step = pl.program_id(1)
    acc_ref = o_ref.at[0]
    last_row = (win - 1) * SUB
    bufs = (buf0_ref, buf1_ref)
    ybufs = (ybuf0_ref, ybuf1_ref)

    def gather(g, buf_ref):
        for j in range(STAGE_LISTS):
            for r in range(GROUP):
                row = pl.multiple_of(jnp.minimum(lst_ref[0, g + j, r], last_row), SUB)
                buf_ref[pl.ds((j * GROUP + r) * SUB, SUB), :] = src_ref[pl.ds(row, SUB), :]

    def scatter(g, ybuf_ref):
        for j in range(STAGE_LISTS):
            for r0 in range(0, GROUP, SCATTER_BATCH):
                rows = [pl.multiple_of(lst_ref[0, g + j, r0 + u], SUB) for u in range(SCATTER_BATCH)]
                vals = [acc_ref[pl.ds(rows[u], SUB), :] + ybuf_ref[pl.ds((j * GROUP + r0 + u) * SUB, SUB), :]
                        for u in range(SCATTER_BATCH)]
                for u in range(SCATTER_BATCH):
                    acc_ref[pl.ds(rows[u], SUB), :] = vals[u]

    def gate_column(g):
        gate_row = jnp.broadcast_to(gl_ref[0, pl.ds(g, 1), :], (GROUP, LANES))
        diag = (lax.broadcasted_iota(jnp.int32, (GROUP, LANES), 0)
                == lax.broadcasted_iota(jnp.int32, (GROUP, LANES), 1))
        return jnp.sum(jnp.where(diag, gate_row, 0.0), axis=1, keepdims=True)

    def expert(g, k, buf_ref, ybuf_ref):
        x = _load_slabs(buf_ref, STAGE_ROWS).astype(BF16)
        a = _silu(_dot(x, wg_ref[k])) * _dot(x, wu_ref[k])
        a = a * jnp.concatenate([gate_column(g + j) for j in range(STAGE_LISTS)], axis=0)
        _store_slabs(ybuf_ref, _dot(a.astype(BF16), wd_ref[k]))

    def stage(g, k, p):
        gather(g + STAGE_LISTS, bufs[1 - p])
        expert(g, k, bufs[p], ybufs[p])
        scatter(g - STAGE_LISTS, ybufs[1 - p])

    def parity(g):
        return (g // STAGE_LISTS) % 2

    @pl.when(step == 0)
    def _():
        o_ref[...] = jnp.zeros_like(o_ref)
        ybuf0_ref[...] = jnp.zeros_like(ybuf0_ref)
        gather(STAGE_LISTS, bufs[1])

    g_end = 0
    for k in range(EXPERTS_PER_STEP):
        e = w * N_EXP + step * EXPERTS_PER_STEP + k
        g0 = first_ref[e]
        n = ngr_ref[e]
        g_end = g0 + n

        def body(i, carry, g0=g0, k=k):
            g = g0 + i * STAGE_LISTS
            for p in range(2):
                pl.when(parity(g) == p)(functools.partial(stage, g, k, p))
            return carry

        lax.fori_loop(0, n // STAGE_LISTS, body, 0)

    @pl.when(step == pl.num_programs(1) - 1)
    def _():
        g_last = g_end - STAGE_LISTS
        for p in range(2):
            pl.when(parity(g_last) == p)(functools.partial(scatter, g_last, ybufs[p]))


def _moe(h2_slabs, ids_t, gates_t, w_gate, w_up, w_down, layer, win):
    n_rows = ids_t.shape[1]
    n_win = n_rows // win
    d, fe = w_gate.shape[2:]
    lists, gate_tab, first, ngr = _tables(ids_t, gates_t, win)
    n_groups = lists.shape[1]
    slab = lambda w, s, first, ngr: (w, 0)
    tab = lambda w, s, first, ngr: (w, 0, 0)
    wspec = lambda w, s, first, ngr: (layer, s, 0, 0)
    eps = EXPERTS_PER_STEP
    return pl.pallas_call(
        functools.partial(_moe_kernel, win),
        grid_spec=pltpu.PrefetchScalarGridSpec(
            num_scalar_prefetch=2,
            grid=(n_win, N_EXP // eps),
            in_specs=[pl.BlockSpec((1, n_groups, LANES), tab, memory_space=pltpu.SMEM),
                      pl.BlockSpec((1, n_groups, LANES), tab),
                      pl.BlockSpec((win * SUB, LANES), slab),
                      pl.BlockSpec((None, eps, d, fe), wspec),
                      pl.BlockSpec((None, eps, d, fe), wspec),
                      pl.BlockSpec((None, eps, fe, d), wspec)],
            out_specs=pl.BlockSpec((1, (win + GROUP) * SUB, LANES), tab),
            scratch_shapes=[pltpu.VMEM((STAGE_ROWS * SUB, LANES), F32)] * 4),
        out_shape=jax.ShapeDtypeStruct((n_win, (win + GROUP) * SUB, LANES), F32),
        compiler_params=_params("arbitrary", "arbitrary"),
        name="moe_routed",
    )(first[:, :, 0].reshape(-1), ngr[:, :, 0].reshape(-1), lists, gate_tab, h2_slabs, w_gate, w_up, w_down)


def _ffn_out_kernel(r_ref, h_ref, x1_ref, mod_ref, gpost_ref, sg_ref, su_ref, sd_ref, o_ref):
    tm = x1_ref.shape[0]
    h = h_ref[...]
    a = _silu(_dot(h, sg_ref[...])) * _dot(h, su_ref[...])
    y = _load_slabs(r_ref.at[0], tm) + _dot(a.astype(BF16), sd_ref[...])
    o_ref[...] = x1_ref[...] + mod_ref[0, 5:6, :] * _rms(y, gpost_ref[...])


def _ffn_out(routed, h2b, x1, mod, g_post, ws_gate, ws_up, ws_down, n_batch, seq, win):
    n_rows, d = x1.shape
    n_lat = n_batch * seq
    tm = FFN_TILE
    per_win = win // tm
    fs = ws_gate.shape[-1]
    row = lambda i: (i, 0)
    const = lambda i: (0, 0)
    return pl.pallas_call(
        _ffn_out_kernel,
        grid=(n_rows // tm,),
        in_specs=[pl.BlockSpec((1, tm * SUB, LANES), lambda i: (i // per_win, i % per_win, 0)),
                  pl.BlockSpec((tm, d), row),
                  pl.BlockSpec((tm, d), row),
                  pl.BlockSpec((1, 6, d), lambda i: (_mod_row(i * tm, n_lat, seq, n_batch), 0, 0)),
                  pl.BlockSpec((1, d), const),
                  pl.BlockSpec((d, fs), const),
                  pl.BlockSpec((d, fs), const),
                  pl.BlockSpec((fs, d), const)],
        out_specs=pl.BlockSpec((tm, d), row),
        out_shape=jax.ShapeDtypeStruct((n_rows, d), F32),
        compiler_params=_params("arbitrary"),
        name="ffn_out",
    )(routed, h2b, x1, mod, g_post, ws_gate, ws_up, ws_down)


def _rope_tables(seq):
    rows = seq // GRID_W
    row = jnp.repeat(jnp.arange(rows), GRID_W).astype(F32)
    col = jnp.tile(jnp.arange(GRID_W), rows).astype(F32)
    n_freq = HEAD_DIM // 4
    inv_freq = ROPE_THETA ** (-jnp.arange(n_freq, dtype=F32) / n_freq)
    ang = jnp.concatenate([row[:, None] * inv_freq, col[:, None] * inv_freq], axis=-1)
    reps = LANES // (HEAD_DIM // 2)
    cos = jnp.tile(jnp.cos(ang), (1, reps))
    sign = jnp.where((jnp.arange(LANES) % HEAD_DIM) < HEAD_DIM // 2, -1.0, 1.0).astype(F32)
    sin = jnp.tile(jnp.sin(ang), (1, reps)) * sign
    cos = jnp.concatenate([cos, jnp.ones((FFN_TILE, LANES), F32)], axis=0)
    sin = jnp.concatenate([sin, jnp.zeros((FFN_TILE, LANES), F32)], axis=0)
    return cos, sin


def _pad_lanes(v):
    return jnp.pad(v.reshape(1, -1), ((0, 0), (0, LANES - v.size)))


def kernel(x, c, ctx, c_ctx, w_mod, b_mod, g_pre_mix, g_post_mix, g_pre_ffn, g_post_ffn, w_in, q_norm, k_norm,
           conv_w, conv_b, dt_bias, a_log, d_skip, ssd_norm, w_out, router_w, router_bias, w_gate, w_up, w_down,
           ws_gate, ws_up, ws_down):
    n_batch, seq, d = x.shape
    ctx_len = ctx.shape[1]
    depth = w_mod.shape[0]
    n_lat = n_batch * seq
    assert n_batch < MOD_ROWS and seq % GRID_W == 0
    assert seq % FFN_TILE == 0 and (n_batch * ctx_len) % FFN_TILE == 0
    assert ctx_len % ROW_TILE == 0 and ctx_len % (CHUNK * SSD_CHUNKS) == 0 and seq % (CHUNK * SSD_CHUNKS) == 0

    xs = jnp.concatenate([x.reshape(n_lat, d), ctx.reshape(n_batch * ctx_len, d)], axis=0)
    cvec = jnp.zeros((MOD_ROWS, d), F32).at[:n_batch].set(c).at[n_batch].set(c_ctx)
    mod_all = _modulation(cvec, w_mod, b_mod)
    cos_t, sin_t = _rope_tables(seq)
    head_lane = jnp.arange(LANES)[:, None]
    chan_head = jnp.arange(SSD_W)[None, :] // SSD_P
    win = math.gcd(MAX_WINDOW, math.gcd(n_lat, n_batch * ctx_len))
    w_gate_b, w_up_b, w_down_b = w_gate.astype(BF16), w_up.astype(BF16), w_down.astype(BF16)

    for i in range(depth):
        last = i == depth - 1
        mod = mod_all[i].reshape(MOD_ROWS, 6, d)
        w_in_b = jnp.pad(w_in[i].astype(BF16), ((0, 0), (0, IN_PAD - IN_W)))
        q, k, v, z, xbc, dt = _inproj(xs, mod, g_pre_mix[i].reshape(1, d), w_in_b,
                                      jnp.tile(q_norm[i], LANES // HEAD_DIM).reshape(1, LANES),
                                      jnp.tile(k_norm[i], LANES // HEAD_DIM).reshape(1, LANES),
                                      cos_t, sin_t, n_batch, seq)
        att = _attention(q, k, v, n_batch, seq, ctx_len, with_ctx=not last)
        dtb = _pad_lanes(dt_bias[i])
        a_neg = _pad_lanes(-jnp.exp(a_log[i]))
        y_f, xc = _ssd(0, xbc, dt, dtb, a_neg, (head_lane == chan_head).astype(BF16), n_batch, seq, ctx_len,
                       (conv_w[i], conv_b[i].reshape(1, CONV_W)))
        extra = (y_f, z, jnp.repeat(d_skip[i], SSD_P).reshape(1, SSD_W), ssd_norm[i].reshape(1, SSD_W))
        ssd = _ssd(1, xc, dt, dtb, a_neg, (head_lane == chan_head + SSD_HEADS).astype(BF16),
                   n_batch, seq, ctx_len, extra)
        n_rows = n_lat if last else xs.shape[0]
        x1, h2, h2b, ids_t, gates_t = _outproj(att, ssd, xs, w_out[i].astype(BF16), mod, g_post_mix[i].reshape(1, d),
                                          g_pre_ffn[i].reshape(1, d), router_w[i].T,
                                          router_bias[i].reshape(N_EXP, 1), n_rows, n_batch, seq)
        routed = _moe(h2, ids_t, gates_t, w_gate_b, w_up_b, w_down_b, i, win)
        xs = _ffn_out(routed, h2b, x1, mod, g_post_ffn[i].reshape(1, d), ws_gate[i].astype(BF16),
                      ws_up[i].astype(BF16), ws_down[i].astype(BF16), n_batch, seq, win)
    return xs.reshape(n_batch, seq, d)
```

```python
import functools
import math

import jax
import jax.numpy as jnp
from jax import lax
from jax.experimental import pallas as pl
from jax.experimental.pallas import tpu as pltpu

F32 = jnp.float32
BF16 = jnp.bfloat16

GRID_W = 64
HEADS = 8
KV_HEADS = 2
HEAD_DIM = 64
Q_PER_KV = HEADS // KV_HEADS
ATT_W = HEADS * HEAD_DIM
ROPE_THETA = 10000.0
SSD_HEADS = 8
SSD_P = 64
SSD_W = SSD_HEADS * SSD_P
SSD_G = 2
SSD_N = 128
CHUNK = 128
SSD_CHUNKS = 2
CONV_W = SSD_W + 2 * SSD_G * SSD_N
IN_W = ATT_W + 2 * KV_HEADS * HEAD_DIM + SSD_W + CONV_W + 2 * SSD_HEADS
LANES = 128
IN_PAD = IN_W - 2 * SSD_HEADS + LANES
N_EXP = 64
TOP_K = 8
N_GROUPS = 8
TOPK_GROUPS = 4
GROUP_SIZE = N_EXP // N_GROUPS
ROUTED_SCALE = 2.5
EPS = 1e-6
MOD_ROWS = 16
ROW_TILE = 256
ATTN_LOOKAHEAD = 1
MAX_WINDOW = 2048
GROUP = 128
STAGE_LISTS = 2
STAGE_ROWS = GROUP * STAGE_LISTS
SCATTER_BATCH = 16
RANK_CHUNK = 256
TOK_SPLIT = 64
FFN_TILE = 512
ROUTER_TILE = 1024
EXPERTS_PER_STEP = 4
SUB = 8
VMEM_LIMIT = 56 * 1024 * 1024
NEG_BIG = -1e30
LOG2_E = math.log2(math.e)

_NT = (((1,), (1,)), ((), ()))


def _params(*sem):
    return pltpu.CompilerParams(dimension_semantics=sem, vmem_limit_bytes=VMEM_LIMIT)


def _sigmoid(v):
    return 1.0 / (1.0 + jnp.exp(-v))


def _silu(v):
    return v * _sigmoid(v)


def _rms(v, gain):
    return v * lax.rsqrt(jnp.mean(v * v, axis=-1, keepdims=True) + EPS) * gain


def _dot(a, b):
    return jnp.dot(a, b, preferred_element_type=F32)


def _split2(v):
    hi = v.astype(BF16)
    lo = (v - hi.astype(F32)).astype(BF16)
    return hi, lo


def _split3(v):
    hi = v.astype(BF16)
    r = v - hi.astype(F32)
    mid = r.astype(BF16)
    lo = (r - mid.astype(F32)).astype(BF16)
    return hi, mid, lo


def _mod_kernel(c_ref, w_ref, b_ref, o_ref):
    s = _silu(c_ref[...])
    o_ref[0] = jnp.dot(s, w_ref[0], preferred_element_type=F32, precision=lax.Precision.HIGHEST) + b_ref[0]


def _modulation(cvec, w_mod, b_mod):
    depth, d, six_d = w_mod.shape
    tn = six_d // 4
    return pl.pallas_call(
        _mod_kernel,
        grid=(depth, six_d // tn),
        in_specs=[pl.BlockSpec((MOD_ROWS, d), lambda l, j: (0, 0)),
                  pl.BlockSpec((1, d, tn), lambda l, j: (l, 0, j)),
                  pl.BlockSpec((1, 1, tn), lambda l, j: (l, 0, j))],
        out_specs=pl.BlockSpec((1, MOD_ROWS, tn), lambda l, j: (l, 0, j)),
        out_shape=jax.ShapeDtypeStruct((depth, MOD_ROWS, six_d), F32),
        compiler_params=_params("arbitrary", "arbitrary"),
        name="modulation",
    )(cvec, w_mod, b_mod.reshape(depth, 1, six_d))


def _head_norm_rope(xb, gain, cos, sin):
    lane = lax.broadcasted_iota(jnp.int32, xb.shape, 1)
    low = lane < HEAD_DIM
    sq = xb * xb
    s_lo = jnp.sum(jnp.where(low, sq, 0.0), axis=-1, keepdims=True)
    s_hi = jnp.sum(jnp.where(low, 0.0, sq), axis=-1, keepdims=True)
    ms = jnp.where(low, s_lo, s_hi) * (1.0 / HEAD_DIM)
    y = xb * lax.rsqrt(ms + EPS) * gain
    ahead = pltpu.roll(y, LANES - HEAD_DIM // 2, 1)
    behind = pltpu.roll(y, HEAD_DIM // 2, 1)
    first_half = (lane % HEAD_DIM) < (HEAD_DIM // 2)
    return y * cos + jnp.where(first_half, ahead, behind) * sin


def _inproj_kernel(x_ref, mod_ref, g_ref, w_ref, qg_ref, kg_ref, cos_ref, sin_ref,
                   q_ref, k_ref, v_ref, z_ref, xbc_ref, dt_ref):
    for sub in range(x_ref.shape[0] // ROW_TILE):
        r = pl.ds(sub * ROW_TILE, ROW_TILE)
        h = _rms(x_ref[r, :], g_ref[...]) * (1.0 + mod_ref[0, 1:2, :]) + mod_ref[0, 0:1, :]
        hb = h.astype(BF16)
        cos = cos_ref[r, :]
        sin = sin_ref[r, :]
        c0 = 0
        for blk in range(ATT_W // LANES):
            qb = _dot(hb, w_ref[:, c0:c0 + LANES])
            qb = _head_norm_rope(qb, qg_ref[...], cos, sin) * (HEAD_DIM ** -0.5 * LOG2_E)
            q_ref[r, c0:c0 + LANES] = qb.astype(q_ref.dtype)
            c0 += LANES
        kb = _head_norm_rope(_dot(hb, w_ref[:, c0:c0 + LANES]), kg_ref[...], cos, sin)
        c0 += LANES
        vb = _dot(hb, w_ref[:, c0:c0 + LANES])
        c0 += LANES
        for g in range(KV_HEADS):
            k_ref[g, r, :] = kb[:, g * HEAD_DIM:(g + 1) * HEAD_DIM].astype(k_ref.dtype)
            v_ref[g, r, :] = vb[:, g * HEAD_DIM:(g + 1) * HEAD_DIM].astype(v_ref.dtype)
        z_ref[r, :] = _dot(hb, w_ref[:, c0:c0 + SSD_W])
        c0 += SSD_W
        xbc_ref[r, :] = _dot(hb, w_ref[:, c0:c0 + CONV_W])
        c0 += CONV_W
        dt_ref[r, :] = _dot(hb, w_ref[:, c0:c0 + LANES])


def _mod_row(start_row, n_lat, seq, n_batch):
    return jnp.where(start_row < n_lat, start_row // seq, n_batch)


def _inproj(xs, mod, g_pre, w_in, q_gain, k_gain, cos_t, sin_t, n_batch, seq):
    n, d = xs.shape
    n_lat = n_batch * seq
    tm = FFN_TILE
    lat_tiles = seq // tm

    def tab_idx(i):
        return (jnp.where(i * tm < n_lat, i % lat_tiles, lat_tiles), 0)

    row = lambda i: (i, 0)
    const = lambda i: (0, 0)
    return pl.pallas_call(
        _inproj_kernel,
        grid=(n // tm,),
        in_specs=[pl.BlockSpec((tm, d), row),
                  pl.BlockSpec((1, 6, d), lambda i: (_mod_row(i * tm, n_lat, seq, n_batch), 0, 0)),
                  pl.BlockSpec((1, d), const),
                  pl.BlockSpec((d, IN_PAD), const),
                  pl.BlockSpec((1, LANES), const),
                  pl.BlockSpec((1, LANES), const),
                  pl.BlockSpec((tm, LANES), tab_idx),
                  pl.BlockSpec((tm, LANES), tab_idx)],
        out_specs=[pl.BlockSpec((tm, ATT_W), row),
                   pl.BlockSpec((KV_HEADS, tm, HEAD_DIM), lambda i: (0, i, 0)),
                   pl.BlockSpec((KV_HEADS, tm, HEAD_DIM), lambda i: (0, i, 0)),
                   pl.BlockSpec((tm, SSD_W), row),
                   pl.BlockSpec((tm, CONV_W), row),
                   pl.BlockSpec((tm, LANES), row)],
        out_shape=[jax.ShapeDtypeStruct((n, ATT_W), BF16),
                   jax.ShapeDtypeStruct((KV_HEADS, n, HEAD_DIM), BF16),
                   jax.ShapeDtypeStruct((KV_HEADS, n, HEAD_DIM), BF16),
                   jax.ShapeDtypeStruct((n, SSD_W), F32),
                   jax.ShapeDtypeStruct((n, CONV_W), F32),
                   jax.ShapeDtypeStruct((n, LANES), F32)],
        compiler_params=_params("arbitrary"),
        name="inproj",
    )(xs, mod, g_pre, w_in, q_gain, k_gain, cos_t, sin_t)


def _attn_heads(q_ref, kc_ref, vc_ref, kl_ref, vl_ref, o_ref, latent):
    outs = []
    q = q_ref[...]

    def scores(h):
        g = h // Q_PER_KV
        qh = q[:, h * HEAD_DIM:(h + 1) * HEAD_DIM]
        sc = lax.dot_general(qh, kc_ref[g], _NT, preferred_element_type=F32)
        sl = lax.dot_general(qh, kl_ref[g], _NT, preferred_element_type=F32) if latent else None
        return sc, sl

    ahead = [scores(h) for h in range(ATTN_LOOKAHEAD)]
    for h in range(HEADS):
        g = h // Q_PER_KV
        sc, sl = ahead.pop(0)
        if h + ATTN_LOOKAHEAD < HEADS:
            ahead.append(scores(h + ATTN_LOOKAHEAD))
        m = jnp.max(sc, axis=-1, keepdims=True)
        if latent:
            m = jnp.maximum(m, jnp.max(sl, axis=-1, keepdims=True))
            pw = jnp.exp2(sl - m)
        pc = jnp.exp2(sc - m)
        den = jnp.sum(pc, axis=-1, keepdims=True)
        acc = _dot(pc.astype(BF16), vc_ref[g])
        if latent:
            den = den + jnp.sum(pw, axis=-1, keepdims=True)
            acc = acc + _dot(pw.astype(BF16), vl_ref[g])
        outs.append(acc / den)
    o_ref[...] = jnp.concatenate(outs, axis=-1).astype(o_ref.dtype)


def _attn_kernel(n_lat_q, with_ctx, q_ref, kc_ref, vc_ref, kl_ref, vl_ref, o_ref):
    if not with_ctx:
        _attn_heads(q_ref, kc_ref, vc_ref, kl_ref, vl_ref, o_ref, True)
        return
    j = pl.program_id(1)

    @pl.when(j < n_lat_q)
    def _():
        _attn_heads(q_ref, kc_ref, vc_ref, kl_ref, vl_ref, o_ref, True)

    @pl.when(j >= n_lat_q)
    def _():
        _attn_heads(q_ref, kc_ref, vc_ref, kl_ref, vl_ref, o_ref, False)


def _attention(q, k, v, n_batch, seq, ctx_len, with_ctx):
    n = q.shape[0]
    n_lat = n_batch * seq
    tq = ROW_TILE
    n_lat_q = seq // tq
    n_ctx_q = ctx_len // tq if with_ctx else 0

    def q_idx(b, j):
        return (jnp.where(j < n_lat_q, b * n_lat_q + j, n_lat // tq + b * (ctx_len // tq) + (j - n_lat_q)), 0)

    ctx_idx = lambda b, j: (0, n_lat // ctx_len + b, 0)
    lat_idx = lambda b, j: (0, b, 0)
    return pl.pallas_call(
        functools.partial(_attn_kernel, n_lat_q, with_ctx),
        grid=(n_batch, n_lat_q + n_ctx_q),
        in_specs=[pl.BlockSpec((tq, ATT_W), q_idx),
                  pl.BlockSpec((KV_HEADS, ctx_len, HEAD_DIM), ctx_idx),
                  pl.BlockSpec((KV_HEADS, ctx_len, HEAD_DIM), ctx_idx),
                  pl.BlockSpec((KV_HEADS, seq, HEAD_DIM), lat_idx),
                  pl.BlockSpec((KV_HEADS, seq, HEAD_DIM), lat_idx)],
        out_specs=pl.BlockSpec((tq, ATT_W), q_idx),
        out_shape=jax.ShapeDtypeStruct((n if with_ctx else n_lat, ATT_W), BF16),
        compiler_params=_params("arbitrary", "arbitrary"),
        name="attention",
    )(q, k, v, k, v)


def _conv_block(npc, npl, u_ref, p_ref, nx_ref, w_ref, b_ref):
    j = pl.program_id(1)
    pos = jnp.where(j < npc, j, j - npc)
    last_pos = jnp.where(j < npc, npc - 1, npl - 1)
    u = u_ref[...]
    rows = u.shape[0]
    r = lax.broadcasted_iota(jnp.int32, u.shape, 0)
    prev_row = jnp.where(pos == 0, 0.0, p_ref[SUB - 1:SUB, :])
    next_row = jnp.where(pos == last_pos, 0.0, nx_ref[0:1, :])
    up = jnp.where(r == 0, prev_row, pltpu.roll(u, 1, 0))
    dn = jnp.where(r == rows - 1, next_row, pltpu.roll(u, rows - 1, 0))
    return _silu(w_ref[0:1, :] * up + w_ref[1:2, :] * u + w_ref[2:3, :] * dn + b_ref[...])


def _ssd_chunks(direction, xcs, dts, dtb, a_neg, ex, st_ref):
    ids = range(len(xcs))
    gw = SSD_W // SSD_G
    hpg = SSD_HEADS // SSD_G
    row = lax.broadcasted_iota(jnp.int32, (CHUNK, CHUNK), 0)
    col = lax.broadcasted_iota(jnp.int32, (CHUNK, CHUNK), 1)
    tri = (col <= row) if direction == 0 else (col >= row)
    tri_b = jnp.where(tri, 1.0, 0.0).astype(BF16)
    edge = CHUNK - 1 if direction == 0 else 0

    xs = [xcs[i][:, 0:SSD_W] for i in ids]
    bgs = [[xcs[i][:, SSD_W + g * SSD_N:SSD_W + (g + 1) * SSD_N] for g in range(SSD_G)] for i in ids]
    cgs = [[xcs[i][:, SSD_W + (SSD_G + g) * SSD_N:SSD_W + (SSD_G + g + 1) * SSD_N].astype(BF16)
            for g in range(SSD_G)] for i in ids]
    grams = [[lax.dot_general(cgs[i][g], bgs[i][g].astype(BF16), _NT, preferred_element_type=F32)
              for g in range(SSD_G)] for i in ids]
    bts = [[bgs[i][g].T.astype(BF16) for g in range(SSD_G)] for i in ids]
    pre = [dts[i] + dtb for i in ids]
    dtv = [jnp.maximum(pre[i], 0.0) + jnp.log1p(jnp.exp(-jnp.abs(pre[i]))) for i in ids]
    splits = [_split3(dtv[i] * a_neg) for i in ids]
    cs = [sum(_dot(tri_b, part) for part in splits[i]) for i in ids]

    def expand(vals):
        parts = [_split2(v) for v in vals]
        return [sum(_dot(part, ex) for part in parts[i]) for i in ids]

    tot = [cs[i][edge:edge + 1, :] for i in ids]
    dt_e = expand(dtv)
    da_e = expand([jnp.exp(cs[i]) for i in ids])
    db_e = expand([jnp.exp(tot[i] - cs[i]) for i in ids])
    xd = [xs[i] * dt_e[i] for i in ids]
    xd_end = [(xd[i] * db_e[i]).astype(BF16) for i in ids]
    cs_t = [cs[i].T for i in ids]
    y_diag = [[] for _ in ids]
    for g in range(SSD_G):
        for hh in range(hpg):
            h = g * hpg + hh
            c = direction * SSD_HEADS + h
            for i in ids:
                diff = cs[i][:, c:c + 1] - cs_t[i][c:c + 1, :]
                decay = jnp.exp(jnp.where(tri, diff, NEG_BIG))
                m = (grams[i][g] * decay).astype(BF16)
                y_diag[i].append(_dot(m, xd[i][:, h * SSD_P:(h + 1) * SSD_P].astype(BF16)))
    st = st_ref[...]
    ys = []
    for i in ids:
        st_b = st.astype(BF16)
        carried = jnp.concatenate([_dot(cgs[i][g], st_b[:, g * gw:(g + 1) * gw]) for g in range(SSD_G)], axis=-1)
        ys.append(jnp.concatenate(y_diag[i], axis=-1) + carried * da_e[i])
        inject = jnp.concatenate([_dot(bts[i][g], xd_end[i][:, g * gw:(g + 1) * gw]) for g in range(SSD_G)], axis=-1)
        st = st * da_e[i][edge:edge + 1, :] + inject
    st_ref[...] = st
    return ys, xs


def _ssd_kernel(direction, npc, npl, *refs):
    if direction == 0:
        (u_ref, p_ref, nx_ref, cw_ref, cb_ref, dt_ref, dtb_ref, a_ref, ex_ref, y_ref, xc_ref, st_ref) = refs
        xc = _conv_block(npc, npl, u_ref, p_ref, nx_ref, cw_ref, cb_ref)
        xc_ref[...] = xc
    else:
        (xc_in_ref, dt_ref, dtb_ref, a_ref, ex_ref, yf_ref, z_ref, dsk_ref, gn_ref, y_ref, st_ref) = refs
        xc = xc_in_ref[...]

    @pl.when(pl.program_id(1) == 0)
    def _():
        st_ref[...] = jnp.zeros_like(st_ref)

    order = list(range(SSD_CHUNKS)) if direction == 0 else list(range(SSD_CHUNKS))[::-1]
    dt = dt_ref[...]
    xcs = [xc[c * CHUNK:(c + 1) * CHUNK] for c in order]
    dts = [dt[c * CHUNK:(c + 1) * CHUNK] for c in order]
    ys, xs = _ssd_chunks(direction, xcs, dts, dtb_ref[...], a_ref[...], ex_ref[...], st_ref)
    for c, y, x in zip(order, ys, xs):
        rows = pl.ds(c * CHUNK, CHUNK)
        if direction == 0:
            y_ref[rows, :] = y
        else:
            gated = (yf_ref[rows, :] + y + x * dsk_ref[...]) * _silu(z_ref[rows, :])
            y_ref[rows, :] = _rms(gated, gn_ref[...]).astype(y_ref.dtype)


def _ssd(direction, xin, dt, dtb, a_neg, ex, n_batch, seq, ctx_len, extra):
    n, cw = xin.shape
    n_lat = n_batch * seq
    blk = CHUNK * SSD_CHUNKS
    npc = ctx_len // blk
    npl = seq // blk
    per = blk // SUB

    def rows(b, j):
        if direction == 0:
            return jnp.where(j < npc, n_lat // blk + b * npc + j, b * npl + (j - npc))
        return jnp.where(j < npc, n_lat // blk + b * npc + (npc - 1 - j), b * npl + (npl - 1 - (j - npc)))

    const = lambda b, j: (0, 0)
    chunk = lambda width: pl.BlockSpec((blk, width), lambda b, j: (rows(b, j), 0))
    scan_specs = [chunk(LANES), pl.BlockSpec((1, LANES), const), pl.BlockSpec((1, LANES), const),
                  pl.BlockSpec((LANES, SSD_W), const)]
    if direction == 0:
        conv_w, conv_b = extra
        in_specs = [chunk(cw),
                    pl.BlockSpec((SUB, cw), lambda b, j: (jnp.maximum(rows(b, j) * per - 1, 0), 0)),
                    pl.BlockSpec((SUB, cw), lambda b, j: (jnp.minimum((rows(b, j) + 1) * per, n // SUB - 1), 0)),
                    pl.BlockSpec((3, cw), const), pl.BlockSpec((1, cw), const)] + scan_specs
        args = [xin, xin, xin, conv_w, conv_b, dt, dtb, a_neg, ex]
        out_specs = [chunk(SSD_W), chunk(cw)]
        out_shape = [jax.ShapeDtypeStruct((n, SSD_W), F32), jax.ShapeDtypeStruct((n, cw), F32)]
    else:
        yf, z, dskip, gnorm = extra
        in_specs = [chunk(cw)] + scan_specs + [chunk(SSD_W), chunk(SSD_W), pl.BlockSpec((1, SSD_W), const),
                                               pl.BlockSpec((1, SSD_W), const)]
        args = [xin, dt, dtb, a_neg, ex, yf, z, dskip, gnorm]
        out_specs = chunk(SSD_W)
        out_shape = jax.ShapeDtypeStruct((n, SSD_W), BF16)
    return pl.pallas_call(
        functools.partial(_ssd_kernel, direction, npc, npl),
        grid=(n_batch, npc + npl),
        in_specs=in_specs,
        out_specs=out_specs,
        out_shape=out_shape,
        scratch_shapes=[pltpu.VMEM((SSD_N, SSD_W), F32)],
        compiler_params=_params("arbitrary", "arbitrary"),
        name="ssd_fwd" if direction == 0 else "ssd_bwd",
    )(*args)


def _route(logits, bias):
    tokens = logits.shape[1]
    scores = _sigmoid(logits)
    choice = scores + bias
    sub = lax.broadcasted_iota(jnp.int32, (GROUP_SIZE, tokens), 0).astype(F32)
    neg = -jnp.inf

    def pick_first_max(v, idx, sentinel):
        m = jnp.max(v, axis=0, keepdims=True)
        first = jnp.min(jnp.where(v == m, idx, sentinel), axis=0, keepdims=True)
        return m, idx == first

    group_scores = []
    for g in range(N_GROUPS):
        cg = choice[g * GROUP_SIZE:(g + 1) * GROUP_SIZE, :]
        m1, hit = pick_first_max(cg, sub, GROUP_SIZE)
        m2 = jnp.max(jnp.where(hit, neg, cg), axis=0, keepdims=True)
        group_scores.append(m1 + m2)
    v = jnp.concatenate(group_scores, axis=0)
    gsel = jnp.zeros_like(v)
    for _ in range(TOPK_GROUPS):
        _, hit = pick_first_max(v, sub, N_GROUPS)
        gsel = jnp.where(hit, 1.0, gsel)
        v = jnp.where(hit, neg, v)
    ok = jnp.concatenate([jnp.broadcast_to(gsel[g:g + 1, :], (GROUP_SIZE, tokens)) for g in range(N_GROUPS)],
                         axis=0) > 0.5
    v = jnp.where(ok, choice, neg)
    eidx = lax.broadcasted_iota(jnp.int32, (N_EXP, tokens), 0).astype(F32)
    ids = []
    gates = []
    for _ in range(TOP_K):
        _, hit = pick_first_max(v, eidx, N_EXP)
        ids.append(jnp.sum(jnp.where(hit, eidx, 0.0), axis=0, keepdims=True))
        gates.append(jnp.sum(jnp.where(hit, scores, 0.0), axis=0, keepdims=True))
        v = jnp.where(hit, neg, v)
    ids = jnp.concatenate(ids, axis=0)
    gates = jnp.concatenate(gates, axis=0)
    gates = gates / jnp.sum(gates, axis=0, keepdims=True) * ROUTED_SCALE
    return ids.astype(jnp.int32), gates


def _store_slabs(slab_ref, rows):
    for kc in range(SUB):
        slab_ref[pl.ds(kc, rows.shape[0], stride=SUB), :] = rows[:, kc * LANES:(kc + 1) * LANES]


def _load_slabs(slab_ref, n):
    return jnp.concatenate([slab_ref[pl.ds(kc, n, stride=SUB), :] for kc in range(SUB)], axis=1)


def _outproj_kernel(att_ref, ssd_ref, x_ref, wo_ref, mod_ref, gpost_ref, gpre_ref, wrh_ref, wrl_ref, rb_ref,
                    x1_ref, h2_ref, h2b_ref, ids_ref, gates_ref):
    m = _dot(att_ref[...], wo_ref[0:ATT_W, :]) + _dot(ssd_ref[...], wo_ref[ATT_W:, :])
    x1 = x_ref[...] + mod_ref[0, 2:3, :] * _rms(m, gpost_ref[...])
    x1_ref[...] = x1
    h2 = _rms(x1, gpre_ref[...]) * (1.0 + mod_ref[0, 4:5, :]) + mod_ref[0, 3:4, :]
    _store_slabs(h2_ref, h2)
    h_hi, h_lo = _split2(h2)
    h2b_ref[...] = h_hi
    nt = lambda a, b: lax.dot_general(a, b, _NT, preferred_element_type=F32)
    logits = nt(wrh_ref[...], h_hi) + (nt(wrh_ref[...], h_lo) + nt(wrl_ref[...], h_hi))
    ids_ref[...], gates_ref[...] = _route(logits, rb_ref[...])


def _outproj(att, ssd, xs, w_out, mod, g_post, g_pre, wr_t, r_bias, n_rows, n_batch, seq):
    d = xs.shape[1]
    n_lat = n_batch * seq
    tm = math.gcd(ROUTER_TILE, math.gcd(seq, n_rows))
    wr_hi = wr_t.astype(BF16)
    wr_lo = (wr_t - wr_hi.astype(F32)).astype(BF16)
    row = lambda i: (i, 0)
    const = lambda i: (0, 0)
    return pl.pallas_call(
        _outproj_kernel,
        grid=(n_rows // tm,),
        in_specs=[pl.BlockSpec((tm, ATT_W), row),
                  pl.BlockSpec((tm, SSD_W), row),
                  pl.BlockSpec((tm, d), row),
                  pl.BlockSpec((ATT_W + SSD_W, d), const),
                  pl.BlockSpec((1, 6, d), lambda i: (_mod_row(i * tm, n_lat, seq, n_batch), 0, 0)),
                  pl.BlockSpec((1, d), const),
                  pl.BlockSpec((1, d), const),
                  pl.BlockSpec((N_EXP, d), const),
                  pl.BlockSpec((N_EXP, d), const),
                  pl.BlockSpec((N_EXP, 1), const)],
        out_specs=[pl.BlockSpec((tm, d), row),
                   pl.BlockSpec((tm * SUB, LANES), row),
                   pl.BlockSpec((tm, d), row),
                   pl.BlockSpec((TOP_K, tm), lambda i: (0, i)),
                   pl.BlockSpec((TOP_K, tm), lambda i: (0, i))],
        out_shape=[jax.ShapeDtypeStruct((n_rows, d), F32),
                   jax.ShapeDtypeStruct((n_rows * SUB, LANES), F32),
                   jax.ShapeDtypeStruct((n_rows, d), BF16),
                   jax.ShapeDtypeStruct((TOP_K, n_rows), jnp.int32),
                   jax.ShapeDtypeStruct((TOP_K, n_rows), F32)],
        compiler_params=_params("arbitrary"),
        name="outproj_router",
    )(att, ssd, xs, w_out, mod, g_post, g_pre, wr_hi, wr_lo, r_bias)


def _tables_kernel(win, n_groups, ids_ref, gates_ref, list_ref, gl_ref, first_ref, ngr_ref):
    ids = ids_ref[...]
    gts = gates_ref[...]
    eidx = lax.broadcasted_iota(jnp.int32, (N_EXP, win), 0)
    hits = [ids[k:k + 1, :] == eidx for k in range(TOP_K)]
    sel = sum(jnp.where(h, 1.0, 0.0) for h in hits)
    cnt = jnp.sum(sel, axis=1, keepdims=True)
    ngr = STAGE_LISTS * jnp.floor((cnt + (STAGE_ROWS - 1)) * (1.0 / STAGE_ROWS))
    er = lax.broadcasted_iota(jnp.int32, (N_EXP, N_EXP), 0)
    ec = lax.broadcasted_iota(jnp.int32, (N_EXP, N_EXP), 1)
    below = jnp.where(ec < er, 1.0, 0.0).astype(BF16)
    ngr_b = jnp.broadcast_to(ngr, (N_EXP, LANES))
    first = _dot(below, ngr_b.astype(BF16)) + STAGE_LISTS
    cr = lax.broadcasted_iota(jnp.int32, (RANK_CHUNK, RANK_CHUNK), 0)
    cc = lax.broadcasted_iota(jnp.int32, (RANK_CHUNK, RANK_CHUNK), 1)
    before = jnp.where(cr < cc, 1.0, 0.0).astype(BF16)
    carry = jnp.zeros((N_EXP, 1), F32)
    ranks = []
    for c0 in range(0, win, RANK_CHUNK):
        s = sel[:, c0:c0 + RANK_CHUNK]
        ranks.append(_dot(s.astype(BF16), before) + carry)
        carry = carry + jnp.sum(s, axis=1, keepdims=True)
    slot_of = first[:, 0:1] * GROUP + jnp.concatenate(ranks, axis=1)
    slot = jnp.concatenate([jnp.sum(jnp.where(h, slot_of, 0.0), axis=0, keepdims=True) for h in hits], axis=0)
    hi = jnp.floor(slot * (1.0 / GROUP))
    lo = slot - hi * GROUP
    g1 = gts.astype(BF16).astype(F32)
    rest = gts - g1
    g2 = rest.astype(BF16).astype(F32)
    g3 = rest - g2
    stack = jnp.concatenate([lo, g1, g2, g3, jnp.zeros((LANES - 4 * TOP_K, win), F32)], axis=0)
    cols = stack.T
    tok1 = lax.broadcasted_iota(jnp.int32, (win, LANES), 0) + 1
    tok_a = (tok1 // TOK_SPLIT).astype(F32)
    tok_b = (tok1 % TOK_SPLIT).astype(F32)
    lane = lax.broadcasted_iota(jnp.int32, (win, LANES), 1).astype(F32)
    gidx = lax.broadcasted_iota(jnp.int32, (n_groups, win), 0).astype(F32)
    acc = jnp.zeros((n_groups, 5 * LANES), F32)
    for k in range(TOP_K):
        onehot = cols[:, k:k + 1] == lane
        vals = (tok_a, tok_b, cols[:, TOP_K + k:TOP_K + k + 1], cols[:, 2 * TOP_K + k:2 * TOP_K + k + 1],
                cols[:, 3 * TOP_K + k:3 * TOP_K + k + 1])
        rhs = jnp.concatenate([jnp.where(onehot, v, 0.0) for v in vals], axis=1).astype(BF16)
        lhs = jnp.where(hi[k:k + 1, :] == gidx, 1.0, 0.0).astype(BF16)
        acc = acc + _dot(lhs, rhs)
    tok1_tab = acc[:, 0:LANES] * TOK_SPLIT + acc[:, LANES:2 * LANES]
    gate_tab = acc[:, 2 * LANES:3 * LANES] + acc[:, 3 * LANES:4 * LANES] + acc[:, 4 * LANES:5 * LANES]
    spare = lax.broadcasted_iota(jnp.int32, (n_groups, LANES), 1).astype(F32) + win
    row_of = jnp.where(tok1_tab > 0.5, tok1_tab - 1.0, spare) * SUB
    list_ref[0] = row_of.astype(jnp.int32)
    gl_ref[0] = gate_tab
    first_ref[0] = first.astype(jnp.int32)
    ngr_ref[0] = ngr_b.astype(jnp.int32)


def _tables(ids_t, gates_t, win):
    n = ids_t.shape[1]
    n_win = n // win
    n_groups = win * TOP_K // GROUP + STAGE_LISTS * N_EXP + SUB
    blk = lambda w: (0, w)
    out = lambda w: (w, 0, 0)
    return pl.pallas_call(
        functools.partial(_tables_kernel, win, n_groups),
        grid=(n_win,),
        in_specs=[pl.BlockSpec((TOP_K, win), blk), pl.BlockSpec((TOP_K, win), blk)],
        out_specs=[pl.BlockSpec((1, n_groups, LANES), out), pl.BlockSpec((1, n_groups, LANES), out),
                   pl.BlockSpec((1, N_EXP, LANES), out), pl.BlockSpec((1, N_EXP, LANES), out)],
        out_shape=[jax.ShapeDtypeStruct((n_win, n_groups, LANES), jnp.int32),
                   jax.ShapeDtypeStruct((n_win, n_groups, LANES), F32),
                   jax.ShapeDtypeStruct((n_win, N_EXP, LANES), jnp.int32),
                   jax.ShapeDtypeStruct((n_win, N_EXP, LANES), jnp.int32)],
        compiler_params=_params("arbitrary"),
        name="dispatch_tables",
    )(ids_t, gates_t)


def _moe_kernel(win, first_ref, ngr_ref, lst_ref, gl_ref, src_ref, wg_ref, wu_ref, wd_ref, o_ref,
                buf0_ref, buf1_ref, ybuf0_ref, ybuf1_ref):
    w = pl.program_id(0)
    step = pl.program_id(1)
    acc_ref = o_ref.at[0]
    last_row = (win - 1) * SUB
    bufs = (buf0_ref, buf1_ref)
    ybufs = (ybuf0_ref, ybuf1_ref)

    def gather(g, buf_ref):
        for j in range(STAGE_LISTS):
            for r in range(GROUP):
                row = pl.multiple_of(jnp.minimum(lst_ref[0, g + j, r], last_row), SUB)
                buf_ref[pl.ds((j * GROUP + r) * SUB, SUB), :] = src_ref[pl.ds(row, SUB), :]

    def scatter(g, ybuf_ref):
        for j in range(STAGE_LISTS):
            for r0 in range(0, GROUP, SCATTER_BATCH):
                rows = [pl.multiple_of(lst_ref[0, g + j, r0 + u], SUB) for u in range(SCATTER_BATCH)]
                vals = [acc_ref[pl.ds(rows[u], SUB), :] + ybuf_ref[pl.ds((j * GROUP + r0 + u) * SUB, SUB), :]
                        for u in range(SCATTER_BATCH)]
                for u in range(SCATTER_BATCH):
                    acc_ref[pl.ds(rows[u], SUB), :] = vals[u]

    def gate_column(g):
        gate_row = jnp.broadcast_to(gl_ref[0, pl.ds(g, 1), :], (GROUP, LANES))
        diag = (lax.broadcasted_iota(jnp.int32, (GROUP, LANES), 0)
                == lax.broadcasted_iota(jnp.int32, (GROUP, LANES), 1))
        return jnp.sum(jnp.where(diag, gate_row, 0.0), axis=1, keepdims=True)

    def expert(g, k, buf_ref, ybuf_ref):
        x = _load_slabs(buf_ref, STAGE_ROWS).astype(BF16)
        a = _silu(_dot(x, wg_ref[k])) * _dot(x, wu_ref[k])
        a = a * jnp.concatenate([gate_column(g + j) for j in range(STAGE_LISTS)], axis=0)
        _store_slabs(ybuf_ref, _dot(a.astype(BF16), wd_ref[k]))

    def stage(g, k, p):
        gather(g + STAGE_LISTS, bufs[1 - p])
        expert(g, k, bufs[p], ybufs[p])
        scatter(g - STAGE_LISTS, ybufs[1 - p])

    def parity(g):
        return (g // STAGE_LISTS) % 2

    @pl.when(step == 0)
    def _():
        o_ref[...] = jnp.zeros_like(o_ref)
        ybuf0_ref[...] = jnp.zeros_like(ybuf0_ref)
        gather(STAGE_LISTS, bufs[1])

    g_end = 0
    for k in range(EXPERTS_PER_STEP):
        e = w * N_EXP + step * EXPERTS_PER_STEP + k
        g0 = first_ref[e]
        n = ngr_ref[e]
        g_end = g0 + n

        def body(i, carry, g0=g0, k=k):
            g = g0 + i * STAGE_LISTS
            for p in range(2):
                pl.when(parity(g) == p)(functools.partial(stage, g, k, p))
            return carry

        lax.fori_loop(0, n // STAGE_LISTS, body, 0)

    @pl.when(step == pl.num_programs(1) - 1)
    def _():
        g_last = g_end - STAGE_LISTS
        for p in range(2):
            pl.when(parity(g_last) == p)(functools.partial(scatter, g_last, ybufs[p]))


def _moe(h2_slabs, ids_t, gates_t, w_gate, w_up, w_down, layer, win):
    n_rows = ids_t.shape[1]
    n_win = n_rows // win
    d, fe = w_gate.shape[2:]
    lists, gate_tab, first, ngr = _tables(ids_t, gates_t, win)
    n_groups = lists.shape[1]
    slab = lambda w, s, first, ngr: (w, 0)
    tab = lambda w, s, first, ngr: (w, 0, 0)
    wspec = lambda w, s, first, ngr: (layer, s, 0, 0)
    eps = EXPERTS_PER_STEP
    return pl.pallas_call(
        functools.partial(_moe_kernel, win),
        grid_spec=pltpu.PrefetchScalarGridSpec(
            num_scalar_prefetch=2,
            grid=(n_win, N_EXP // eps),
            in_specs=[pl.BlockSpec((1, n_groups, LANES), tab, memory_space=pltpu.SMEM),
                      pl.BlockSpec((1, n_groups, LANES), tab),
                      pl.BlockSpec((win * SUB, LANES), slab),
                      pl.BlockSpec((None, eps, d, fe), wspec),
                      pl.BlockSpec((None, eps, d, fe), wspec),
                      pl.BlockSpec((None, eps, fe, d), wspec)],
            out_specs=pl.BlockSpec((1, (win + GROUP) * SUB, LANES), tab),
            scratch_shapes=[pltpu.VMEM((STAGE_ROWS * SUB, LANES), F32)] * 4),
        out_shape=jax.ShapeDtypeStruct((n_win, (win + GROUP) * SUB, LANES), F32),
        compiler_params=_params("arbitrary", "arbitrary"),
        name="moe_routed",
    )(first[:, :, 0].reshape(-1), ngr[:, :, 0].reshape(-1), lists, gate_tab, h2_slabs, w_gate, w_up, w_down)


def _ffn_out_kernel(r_ref, h_ref, x1_ref, mod_ref, gpost_ref, sg_ref, su_ref, sd_ref, o_ref):
    tm = x1_ref.shape[0]
    h = h_ref[...]
    a = _silu(_dot(h, sg_ref[...])) * _dot(h, su_ref[...])
    y = _load_slabs(r_ref.at[0], tm) + _dot(a.astype(BF16), sd_ref[...])
    o_ref[...] = x1_ref[...] + mod_ref[0, 5:6, :] * _rms(y, gpost_ref[...])


def _ffn_out(routed, h2b, x1, mod, g_post, ws_gate, ws_up, ws_down, n_batch, seq, win):
    n_rows, d = x1.shape
    n_lat = n_batch * seq
    tm = FFN_TILE
    per_win = win // tm
    fs = ws_gate.shape[-1]
    row = lambda i: (i, 0)
    const = lambda i: (0, 0)
    return pl.pallas_call(
        _ffn_out_kernel,
        grid=(n_rows // tm,),
        in_specs=[pl.BlockSpec((1, tm * SUB, LANES), lambda i: (i // per_win, i % per_win, 0)),
                  pl.BlockSpec((tm, d), row),
                  pl.BlockSpec((tm, d), row),
                  pl.BlockSpec((1, 6, d), lambda i: (_mod_row(i * tm, n_lat, seq, n_batch), 0, 0)),
                  pl.BlockSpec((1, d), const),
                  pl.BlockSpec((d, fs), const),
                  pl.BlockSpec((d, fs), const),
                  pl.BlockSpec((fs, d), const)],
        out_specs=pl.BlockSpec((tm, d), row),
        out_shape=jax.ShapeDtypeStruct((n_rows, d), F32),
        compiler_params=_params("arbitrary"),
        name="ffn_out",
    )(routed, h2b, x1, mod, g_post, ws_gate, ws_up, ws_down)


def _rope_tables(seq):
    rows = seq // GRID_W
    row = jnp.repeat(jnp.arange(rows), GRID_W).astype(F32)
    col = jnp.tile(jnp.arange(GRID_W), rows).astype(F32)
    n_freq = HEAD_DIM // 4
    inv_freq = ROPE_THETA ** (-jnp.arange(n_freq, dtype=F32) / n_freq)
    ang = jnp.concatenate([row[:, None] * inv_freq, col[:, None] * inv_freq], axis=-1)
    reps = LANES // (HEAD_DIM // 2)
    cos = jnp.tile(jnp.cos(ang), (1, reps))
    sign = jnp.where((jnp.arange(LANES) % HEAD_DIM) < HEAD_DIM // 2, -1.0, 1.0).astype(F32)
    sin = jnp.tile(jnp.sin(ang), (1, reps)) * sign
    cos = jnp.concatenate([cos, jnp.ones((FFN_TILE, LANES), F32)], axis=0)
    sin = jnp.concatenate([sin, jnp.zeros((FFN_TILE, LANES), F32)], axis=0)
    return cos, sin


def _pad_lanes(v):
    return jnp.pad(v.reshape(1, -1), ((0, 0), (0, LANES - v.size)))


def kernel(x, c, ctx, c_ctx, w_mod, b_mod, g_pre_mix, g_post_mix, g_pre_ffn, g_post_ffn, w_in, q_norm, k_norm,
           conv_w, conv_b, dt_bias, a_log, d_skip, ssd_norm, w_out, router_w, router_bias, w_gate, w_up, w_down,
           ws_gate, ws_up, ws_down):
    n_batch, seq, d = x.shape
    ctx_len = ctx.shape[1]
    depth = w_mod.shape[0]
    n_lat = n_batch * seq
    assert n_batch < MOD_ROWS and seq % GRID_W == 0
    assert seq % FFN_TILE == 0 and (n_batch * ctx_len) % FFN_TILE == 0
    assert ctx_len % ROW_TILE == 0 and ctx_len % (CHUNK * SSD_CHUNKS) == 0 and seq % (CHUNK * SSD_CHUNKS) == 0

    xs = jnp.concatenate([x.reshape(n_lat, d), ctx.reshape(n_batch * ctx_len, d)], axis=0)
    cvec = jnp.zeros((MOD_ROWS, d), F32).at[:n_batch].set(c).at[n_batch].set(c_ctx)
    mod_all = _modulation(cvec, w_mod, b_mod)
    cos_t, sin_t = _rope_tables(seq)
    head_lane = jnp.arange(LANES)[:, None]
    chan_head = jnp.arange(SSD_W)[None, :] // SSD_P
    win = math.gcd(MAX_WINDOW, math.gcd(n_lat, n_batch * ctx_len))
    w_gate_b, w_up_b, w_down_b = w_gate.astype(BF16), w_up.astype(BF16), w_down.astype(BF16)

    for i in range(depth):
        last = i == depth - 1
        mod = mod_all[i].reshape(MOD_ROWS, 6, d)
        w_in_b = jnp.pad(w_in[i].astype(BF16), ((0, 0), (0, IN_PAD - IN_W)))
        q, k, v, z, xbc, dt = _inproj(xs, mod, g_pre_mix[i].reshape(1, d), w_in_b,
                                      jnp.tile(q_norm[i], LANES // HEAD_DIM).reshape(1, LANES),
                                      jnp.tile(k_norm[i], LANES // HEAD_DIM).reshape(1, LANES),
                                      cos_t, sin_t, n_batch, seq)
        att = _attention(q, k, v, n_batch, seq, ctx_len, with_ctx=not last)
        dtb = _pad_lanes(dt_bias[i])
        a_neg = _pad_lanes(-jnp.exp(a_log[i]))
        y_f, xc = _ssd(0, xbc, dt, dtb, a_neg, (head_lane == chan_head).astype(BF16), n_batch, seq, ctx_len,
                       (conv_w[i], conv_b[i].reshape(1, CONV_W)))
        extra = (y_f, z, jnp.repeat(d_skip[i], SSD_P).reshape(1, SSD_W), ssd_norm[i].reshape(1, SSD_W))
        ssd = _ssd(1, xc, dt, dtb, a_neg, (head_lane == chan_head + SSD_HEADS).astype(BF16),
                   n_batch, seq, ctx_len, extra)
        n_rows = n_lat if last else xs.shape[0]
        x1, h2, h2b, ids_t, gates_t = _outproj(att, ssd, xs, w_out[i].astype(BF16), mod, g_post_mix[i].reshape(1, d),
                                          g_pre_ffn[i].reshape(1, d), router_w[i].T,
                                          router_bias[i].reshape(N_EXP, 1), n_rows, n_batch, seq)
        routed = _moe(h2, ids_t, gates_t, w_gate_b, w_up_b, w_down_b, i, win)
        xs = _ffn_out(routed, h2b, x1, mod, g_post_ffn[i].reshape(1, d), ws_gate[i].astype(BF16),
                      ws_up[i].astype(BF16), ws_down[i].astype(BF16), n_batch, seq, win)
    return xs.reshape(n_batch, seq, d)
```
